```python
import jax
import jax.numpy as jnp
from jax import lax
import numpy as np

D_MODEL = 1024
BATCH = 8
SEQ = 2048
DEPTH = 2

GRID_W = 64
CTX_LEN = 256
HEAD_DIM = 64
MLA_HEADS = D_MODEL // (2 * HEAD_DIM)
MLA_Q_RANK = 3 * D_MODEL // 8
MLA_KV_RANK = D_MODEL // 4
MLA_NOPE = HEAD_DIM
MLA_ROPE = HEAD_DIM // 2
MLA_QK = MLA_NOPE + MLA_ROPE
MLA_V = HEAD_DIM
NA_HEADS = D_MODEL // (2 * HEAD_DIM)
NA_WIN_ROWS = 8
NA_WIN_COLS = 16
D_IN = MLA_Q_RANK + MLA_KV_RANK + MLA_ROPE + 3 * NA_HEADS * HEAD_DIM
ROPE_THETA = 10000.0
Q_BLOCK = 128
CONV_WIDTH = 31
PEER_HEADS = 8
PEER_N_KEYS = 128
PEER_N_EXPERTS = PEER_N_KEYS * PEER_N_KEYS
PEER_TOPK = 16
PEER_KEY_DIM = 256
PEER_TOKEN_BLOCK = 128
EPS = 1e-6

kernel_name = "hybrid_mla_natten_conformer_peer_dit"


def rms_norm(x, g):
    xf = x.astype(jnp.float32)
    y = xf * lax.rsqrt(jnp.mean(xf * xf, axis=-1, keepdims=True) + EPS)
    return (y * g.astype(jnp.float32)).astype(x.dtype)


def layer_norm(x, g, b):
    xf = x.astype(jnp.float32)
    mu = jnp.mean(xf, axis=-1, keepdims=True)
    var = jnp.mean(jnp.square(xf - mu), axis=-1, keepdims=True)
    y = (xf - mu) * lax.rsqrt(var + EPS)
    return (y * g.astype(jnp.float32) + b.astype(jnp.float32)).astype(x.dtype)


def modulate(x, shift, scale):
    return x * (1 + scale[:, None, :]) + shift[:, None, :]


def ada_params(cond, w, b):
    return jnp.split(jax.nn.silu(cond) @ w + b, 6, axis=-1)


def axial_rope_tables(n_tokens, rot_dim):
    t = jnp.arange(n_tokens)
    row = (t // GRID_W).astype(jnp.float32)
    col = (t % GRID_W).astype(jnp.float32)
    n_freq = rot_dim // 4
    inv_freq = 1.0 / (ROPE_THETA ** (jnp.arange(n_freq, dtype=jnp.float32) / n_freq))
    ang = jnp.concatenate([row[:, None] * inv_freq, col[:, None] * inv_freq], axis=-1)
    return jnp.cos(ang)[:, None, :], jnp.sin(ang)[:, None, :]


def apply_rope(x, cos, sin):
    x1, x2 = jnp.split(x, 2, axis=-1)
    cos = cos.astype(x.dtype)
    sin = sin.astype(x.dtype)
    return jnp.concatenate([x1 * cos - x2 * sin, x1 * sin + x2 * cos], axis=-1)


def to_heads(x):
    return x.transpose(0, 2, 1, 3)


def merge_heads(o):
    b, h, t, d = o.shape
    return o.transpose(0, 2, 1, 3).reshape(b, t, h * d)


def split_combined(z):
    a = MLA_Q_RANK
    b = a + MLA_KV_RANK
    c = b + MLA_ROPE
    return z[..., :a], z[..., a:b], z[..., b:c], z[..., c:]


def mla_queries(cq, g_qa, w_q_up, g_q, rope):
    b, t, _ = cq.shape
    q = (rms_norm(cq, g_qa) @ w_q_up).reshape(b, t, MLA_HEADS, MLA_QK)
    q = rms_norm(q, g_q)
    if rope is not None:
        q = jnp.concatenate([q[..., :MLA_NOPE], apply_rope(q[..., MLA_NOPE:], *rope)], axis=-1)
    return to_heads(q)


def mla_keys_values(ckv, k_rope, g_kva, w_kv_up, g_k, rope):
    b, t, _ = ckv.shape
    kv = (rms_norm(ckv, g_kva) @ w_kv_up).reshape(b, t, MLA_HEADS, MLA_NOPE + MLA_V)
    k_rope = jnp.broadcast_to(k_rope[:, :, None, :], (b, t, MLA_HEADS, MLA_ROPE))
    k = rms_norm(jnp.concatenate([kv[..., :MLA_NOPE], k_rope], axis=-1), g_k)
    if rope is not None:
        k = jnp.concatenate([k[..., :MLA_NOPE], apply_rope(k[..., MLA_NOPE:], *rope)], axis=-1)
    return to_heads(k), to_heads(kv[..., MLA_NOPE:])


def na_split(z_na):
    b, t, _ = z_na.shape
    qkv = z_na.reshape(b, t, 3, NA_HEADS, HEAD_DIM)
    return qkv[:, :, 0], qkv[:, :, 1], qkv[:, :, 2]


def latent_global_attention(q, k_lat, v_lat, k_ctx, v_ctx):
    b, h, s, dq = q.shape
    k = jnp.concatenate([k_ctx, k_lat], axis=2)
    v = jnp.concatenate([v_ctx, v_lat], axis=2)
    scale = dq ** -0.5
    nb = s // Q_BLOCK
    q_blocks = q.reshape(b, h, nb, Q_BLOCK, dq).transpose(2, 0, 1, 3, 4)

    def block(qb):
        sc = jnp.einsum("bhqd,bhkd->bhqk", qb, k).astype(jnp.float32) * scale
        p = jax.nn.softmax(sc, axis=-1).astype(v.dtype)
        return jnp.einsum("bhqk,bhkd->bhqd", p, v)

    o = lax.map(block, q_blocks)
    return o.transpose(1, 2, 0, 3, 4).reshape(b, h, s, v.shape[-1])


def neighbourhood_index(s):
    rows = s // GRID_W
    wr = min(NA_WIN_ROWS, rows)
    t = jnp.arange(s)
    r, col = t // GRID_W, t % GRID_W
    r0 = jnp.clip(r - wr // 2, 0, rows - wr)
    c0 = jnp.clip(col - NA_WIN_COLS // 2, 0, GRID_W - NA_WIN_COLS)
    kr = r0[:, None, None] + jnp.arange(wr)[None, :, None]
    kc = c0[:, None, None] + jnp.arange(NA_WIN_COLS)[None, None, :]
    nk = wr * NA_WIN_COLS
    idx = (kr * GRID_W + kc).reshape(s, nk)
    rel_r = jnp.broadcast_to(kr - r[:, None, None] + NA_WIN_ROWS - 1, (s, wr, NA_WIN_COLS)).reshape(s, nk)
    rel_c = jnp.broadcast_to(kc - col[:, None, None] + NA_WIN_COLS - 1, (s, wr, NA_WIN_COLS)).reshape(s, nk)
    return idx, rel_r, rel_c


def neighbourhood_attention(q, k, v, k_ctx, v_ctx, rpb):
    b, h, s, dh = q.shape
    idx, rel_r, rel_c = neighbourhood_index(s)
    nk = idx.shape[-1]
    nb = s // GRID_W
    scale = dh ** -0.5
    q_blocks = q.reshape(b, h, nb, GRID_W, dh).transpose(2, 0, 1, 3, 4)

    def block(args):
        qb, ib, rb, cb = args
        k_sel = k[:, :, ib]
        v_sel = v[:, :, ib]
        s_loc = (jnp.einsum("bhqd,bhqkd->bhqk", qb, k_sel).astype(jnp.float32) * scale
                 + rpb[:, rb, cb].astype(jnp.float32))
        s_ctx = jnp.einsum("bhqd,bhkd->bhqk", qb, k_ctx).astype(jnp.float32) * scale
        p = jax.nn.softmax(jnp.concatenate([s_loc, s_ctx], axis=-1), axis=-1).astype(v.dtype)
        return (jnp.einsum("bhqk,bhqkd->bhqd", p[..., :nk], v_sel)
                + jnp.einsum("bhqk,bhkd->bhqd", p[..., nk:], v_ctx))

    o = lax.map(block, (q_blocks, idx.reshape(nb, GRID_W, nk),
                        rel_r.reshape(nb, GRID_W, nk), rel_c.reshape(nb, GRID_W, nk)))
    return o.transpose(1, 2, 0, 3, 4).reshape(b, h, s, dh)


def context_attention(q, k, v):
    sc = jnp.einsum("bhqd,bhkd->bhqk", q, k).astype(jnp.float32) * (q.shape[-1] ** -0.5)
    p = jax.nn.softmax(sc, axis=-1).astype(v.dtype)
    return jnp.einsum("bhqk,bhkd->bhqd", p, v)


def attention_mixer(a_lat, a_ctx, w_in, g_qa, w_q_up, g_kva, w_kv_up, g_q, g_k,
                    na_g_q, na_g_k, rpb, w_out, rope, ctx_out):
    cq_l, ckv_l, kr_l, na_l = split_combined(a_lat @ w_in)
    cq_c, ckv_c, kr_c, na_c = split_combined(a_ctx @ w_in)
    q_m = mla_queries(cq_l, g_qa, w_q_up, g_q, rope)
    k_m, v_m = mla_keys_values(ckv_l, kr_l, g_kva, w_kv_up, g_k, rope)
    k_mc, v_mc = mla_keys_values(ckv_c, kr_c, g_kva, w_kv_up, g_k, None)
    q_n, k_n, v_n = na_split(na_l)
    qc_n, kc_n, vc_n = na_split(na_c)
    q_n = to_heads(rms_norm(q_n, na_g_q))
    k_n = to_heads(rms_norm(k_n, na_g_k))
    v_n = to_heads(v_n)
    k_nc = to_heads(rms_norm(kc_n, na_g_k))
    v_nc = to_heads(vc_n)
    o_m = latent_global_attention(q_m, k_m, v_m, k_mc, v_mc)
    o_n = neighbourhood_attention(q_n, k_n, v_n, k_nc, v_nc, rpb)
    y_lat = jnp.concatenate([merge_heads(o_m), merge_heads(o_n)], axis=-1) @ w_out
    if not ctx_out:
        return y_lat, None
    q_mc = mla_queries(cq_c, g_qa, w_q_up, g_q, None)
    q_nc = to_heads(rms_norm(qc_n, na_g_q))
    y_ctx = jnp.concatenate([merge_heads(context_attention(q_mc, k_mc, v_mc)),
                             merge_heads(context_attention(q_nc, k_nc, v_nc))], axis=-1) @ w_out
    return y_lat, y_ctx


def conformer_conv(h, w_pw1, b_pw1, w_dw, b_dw, g_ln, b_ln, w_pw2, b_pw2):
    d = h.shape[-1]
    a = h @ w_pw1 + b_pw1
    a = a[..., :d] * jax.nn.sigmoid(a[..., d:])
    a = lax.conv_general_dilated(a, w_dw[:, None, :], window_strides=(1,),
                                 padding=[(CONV_WIDTH // 2, CONV_WIDTH // 2)],
                                 dimension_numbers=("NWC", "WIO", "NWC"),
                                 feature_group_count=d) + b_dw
    a = jax.nn.silu(layer_norm(a, g_ln, b_ln))
    return a @ w_pw2 + b_pw2


def peer_ffn(h, w_query, sub_keys, u_tab, v_tab):
    b, t, d = h.shape
    half = PEER_KEY_DIM // 2
    h_blocks = h.reshape(-1, PEER_TOKEN_BLOCK, d)

    def block(hb):
        q = (hb @ w_query).reshape(PEER_TOKEN_BLOCK, PEER_HEADS, 2, half)
        s = jnp.einsum("thpd,hpnd->thpn", q, sub_keys).astype(jnp.float32)
        s_top, i_top = lax.top_k(s, PEER_TOPK)
        cand = (s_top[:, :, 0, :, None] + s_top[:, :, 1, None, :]).reshape(
            PEER_TOKEN_BLOCK, PEER_HEADS, PEER_TOPK * PEER_TOPK)
        cand_idx = (i_top[:, :, 0, :, None] * PEER_N_KEYS + i_top[:, :, 1, None, :]).reshape(
            PEER_TOKEN_BLOCK, PEER_HEADS, PEER_TOPK * PEER_TOPK)
        best, pos = lax.top_k(cand, PEER_TOPK)
        expert = jnp.take_along_axis(cand_idx, pos, axis=-1)
        gate = jax.nn.softmax(best, axis=-1)
        u = jnp.take(u_tab, expert, axis=0)
        act = jax.nn.gelu(jnp.einsum("thkd,td->thk", u, hb).astype(jnp.float32), approximate=False)
        w = (gate * act).astype(hb.dtype)
        return jnp.einsum("thk,thkd->td", w, jnp.take(v_tab, expert, axis=0))

    return lax.map(block, h_blocks).reshape(b, t, d)


def setup_inputs(seed: int = 0) -> dict:
    key = jax.random.key(seed)
    ks = iter(jax.random.split(key, 32))
    f32 = jnp.float32
    d = D_MODEL
    n_attn = (DEPTH + 1) // 2
    n_conv = DEPTH // 2

    def normal(shape, scale):
        return jax.random.normal(next(ks), shape, f32) * scale

    def gain(shape):
        return 1.0 + 0.05 * jax.random.normal(next(ks), shape, f32)

    inp = {}
    inp["x"] = normal((BATCH, SEQ, d), 1.0)
    inp["c"] = normal((BATCH, d), 1.0)
    inp["ctx"] = normal((BATCH, CTX_LEN, d), 1.0)
    inp["c_ctx"] = normal((d,), 1.0)
    inp["w_ada"] = normal((DEPTH, d, 6 * d), 0.5 * d ** -0.5)
    inp["b_ada"] = normal((DEPTH, 6 * d), 0.02)
    inp["g_norm1"] = gain((DEPTH, d))
    inp["g_norm2"] = gain((DEPTH, d))
    inp["attn_w_in"] = normal((n_attn, d, D_IN), d ** -0.5)
    inp["mla_g_qa"] = gain((n_attn, MLA_Q_RANK))
    inp["mla_w_q_up"] = normal((n_attn, MLA_Q_RANK, MLA_HEADS * MLA_QK), MLA_Q_RANK ** -0.5)
    inp["mla_g_kva"] = gain((n_attn, MLA_KV_RANK))
    inp["mla_w_kv_up"] = normal((n_attn, MLA_KV_RANK, MLA_HEADS * (MLA_NOPE + MLA_V)), MLA_KV_RANK ** -0.5)
    inp["mla_g_q"] = gain((n_attn, MLA_QK))
    inp["mla_g_k"] = gain((n_attn, MLA_QK))
    inp["na_g_q"] = gain((n_attn, HEAD_DIM))
    inp["na_g_k"] = gain((n_attn, HEAD_DIM))
    inp["na_rpb"] = normal((n_attn, NA_HEADS, 2 * NA_WIN_ROWS - 1, 2 * NA_WIN_COLS - 1), 0.2)
    inp["attn_w_out"] = normal((n_attn, d, d), d ** -0.5)
    inp["conv_w_pw1"] = normal((n_conv, d, 2 * d), d ** -0.5)
    inp["conv_b_pw1"] = normal((n_conv, 2 * d), 0.02)
    inp["conv_w_dw"] = normal((n_conv, CONV_WIDTH, d), CONV_WIDTH ** -0.5)
    inp["conv_b_dw"] = normal((n_conv, d), 0.02)
    inp["conv_g_ln"] = gain((n_conv, d))
    inp["conv_b_ln"] = normal((n_conv, d), 0.02)
    inp["conv_w_pw2"] = normal((n_conv, d, d), d ** -0.5)
    inp["conv_b_pw2"] = normal((n_conv, d), 0.02)
    inp["peer_w_query"] = normal((DEPTH, d, PEER_HEADS * PEER_KEY_DIM), d ** -0.5)
    inp["peer_sub_keys"] = normal((DEPTH, PEER_HEADS, 2, PEER_N_KEYS, PEER_KEY_DIM // 2), (PEER_KEY_DIM // 2) ** -0.5)
    inp["peer_u"] = normal((DEPTH, PEER_N_EXPERTS, d), d ** -0.5)
    inp["peer_v"] = normal((DEPTH, PEER_N_EXPERTS, d), PEER_HEADS ** -0.5)
    return inp


def reference(x, c, ctx, c_ctx, w_ada, b_ada, g_norm1, g_norm2,
              attn_w_in, mla_g_qa, mla_w_q_up, mla_g_kva, mla_w_kv_up, mla_g_q, mla_g_k,
              na_g_q, na_g_k, na_rpb, attn_w_out,
              conv_w_pw1, conv_b_pw1, conv_w_dw, conv_b_dw, conv_g_ln, conv_b_ln, conv_w_pw2, conv_b_pw2,
              peer_w_query, peer_sub_keys, peer_u, peer_v):
    rope = axial_rope_tables(x.shape[1], MLA_ROPE)
    c_ctx_row = c_ctx[None, :]
    h_lat, h_ctx = x, ctx
    for l in range(DEPTH):
        ctx_live = any(j % 2 == 0 for j in range(l + 1, DEPTH))
        ctx_in = (l % 2 == 0) or ctx_live
        sh1, sc1, g1, sh2, sc2, g2 = ada_params(c, w_ada[l], b_ada[l])
        a_lat = modulate(rms_norm(h_lat, g_norm1[l]), sh1, sc1)
        a_ctx = None
        if ctx_in:
            sh1c, sc1c, g1c, sh2c, sc2c, g2c = ada_params(c_ctx_row, w_ada[l], b_ada[l])
            a_ctx = modulate(rms_norm(h_ctx, g_norm1[l]), sh1c, sc1c)
        i = l // 2
        if l % 2 == 0:
            y_lat, y_ctx = attention_mixer(a_lat, a_ctx, attn_w_in[i], mla_g_qa[i], mla_w_q_up[i],
                                           mla_g_kva[i], mla_w_kv_up[i], mla_g_q[i], mla_g_k[i],
                                           na_g_q[i], na_g_k[i], na_rpb[i], attn_w_out[i], rope, ctx_live)
        else:
            conv_p = (conv_w_pw1[i], conv_b_pw1[i], conv_w_dw[i], conv_b_dw[i],
                      conv_g_ln[i], conv_b_ln[i], conv_w_pw2[i], conv_b_pw2[i])
            y_lat = conformer_conv(a_lat, *conv_p)
            y_ctx = conformer_conv(a_ctx, *conv_p) if ctx_live else None
        peer_p = (peer_w_query[l], peer_sub_keys[l], peer_u[l], peer_v[l])
        h_lat = h_lat + g1[:, None, :] * y_lat
        h_lat = h_lat + g2[:, None, :] * peer_ffn(modulate(rms_norm(h_lat, g_norm2[l]), sh2, sc2), *peer_p)
        if ctx_live:
            h_ctx = h_ctx + g1c[:, None, :] * y_ctx
            h_ctx = h_ctx + g2c[:, None, :] * peer_ffn(modulate(rms_norm(h_ctx, g_norm2[l]), sh2c, sc2c), *peer_p)
    return h_lat
```

```python
import functools

import jax
import jax.numpy as jnp
import numpy as np
from jax import lax
from jax.experimental import pallas as pl
from jax.experimental.pallas import tpu as pltpu

F32 = jnp.float32
BF16 = jnp.bfloat16

LANES = 128
EPS = 1e-6
NEG_BIG = -1e30
VMEM_LIMIT = 56 * 1024 * 1024

GRID_W = 64
HEAD_DIM = 64
N_HEADS = 8
MLA_NOPE = 64
MLA_ROPE = 32
MLA_QK = MLA_NOPE + MLA_ROPE
NA_WIN_ROWS = 8
NA_WIN_COLS = 16
ROPE_THETA = 10000.0
CONV_WIDTH = 31
CONV_HALO = 16
PEER_HEADS = 8
PEER_N_KEYS = 128
PEER_TOPK = 16

NT_DIMS = (((1,), (1,)), ((), ()))


def _params(*sem):
    return pltpu.CompilerParams(dimension_semantics=sem, vmem_limit_bytes=VMEM_LIMIT)


def _mm(a, b):
    return jnp.dot(a, b, preferred_element_type=F32)


def _mm_nt(a, b):
    return lax.dot_general(a, b, NT_DIMS, preferred_element_type=F32)


def _rms_mod(x, g, shift, scale):
    inv = lax.rsqrt(jnp.mean(x * x, axis=-1, keepdims=True) + EPS)
    return (x * inv * g) * (1.0 + scale) + shift


def _ada_kernel(cond_ref, w_ref, b_ref, out_ref):
    cond = cond_ref[...]
    a = cond * jax.nn.sigmoid(cond)
    out_ref[...] = jnp.dot(a, w_ref[...], preferred_element_type=F32,
                           precision=lax.Precision.HIGHEST) + b_ref[...]


def ada_params(cond, w_ada, b_ada, *, tn=1536):
    depth, d, n = w_ada.shape
    rows = cond.shape[0]
    return pl.pallas_call(
        _ada_kernel,
        grid=(depth, n // tn),
        in_specs=[pl.BlockSpec((rows, d), lambda l, j: (0, 0)),
                  pl.BlockSpec((None, d, tn), lambda l, j: (l, 0, j)),
                  pl.BlockSpec((None, 1, tn), lambda l, j: (l, 0, j))],
        out_specs=pl.BlockSpec((None, rows, tn), lambda l, j: (l, 0, j)),
        out_shape=jax.ShapeDtypeStruct((depth, rows, n), F32),
        compiler_params=_params("parallel", "parallel"),
        name="ada_params",
    )(cond, w_ada, b_ada.reshape(depth, 1, n))


def _rope128(x, cos, sin_lo, sin_hi):
    return x * cos + pltpu.roll(x, LANES - MLA_ROPE // 2, 1) * sin_lo + pltpu.roll(x, MLA_ROPE // 2, 1) * sin_hi


def _head_rms128(x, n_valid):
    ss = jnp.sum(x * x, axis=-1, keepdims=True) * (1.0 / n_valid)
    return x * lax.rsqrt(ss + EPS)


def _pair_rms64(x, lane_lo):
    xx = x * x
    ss_lo = jnp.sum(jnp.where(lane_lo, xx, 0.0), axis=-1, keepdims=True)
    ss_hi = jnp.sum(jnp.where(lane_lo, 0.0, xx), axis=-1, keepdims=True)
    inv = jnp.where(lane_lo, lax.rsqrt(ss_lo * (1.0 / HEAD_DIM) + EPS), lax.rsqrt(ss_hi * (1.0 / HEAD_DIM) + EPS))
    return x * inv


def _attn_in_kernel(x_ref, g1_ref, sh_ref, sc_ref, w_in_ref, gqa_ref, wq_ref, gkva_ref, wk_ref, wv_ref,
                    gq_ref, gk_ref, gnq_ref, gnk_ref, cos_ref, slo_ref, shi_ref, *out_refs, latent):
    if latent:
        qm_ref, km_ref, vm_ref, qn_ref, kn_ref, vn_ref = out_refs
    else:
        km_ref, vm_ref, kn_ref, vn_ref = out_refs
    a = _rms_mod(x_ref[...], g1_ref[...], sh_ref[...], sc_ref[...]).astype(BF16)
    z = _mm(a, w_in_ref[...])
    c0, c1, c2 = 384, 384 + 256, 384 + 256 + LANES
    cq, ckv, kr = z[:, :c0], z[:, c0:c1], z[:, c1:c2]
    naq, nak, nav = z[:, c2:c2 + 512], z[:, c2 + 512:c2 + 1024], z[:, c2 + 1024:c2 + 1536]

    def rms(v, g):
        return v * lax.rsqrt(jnp.mean(v * v, axis=-1, keepdims=True) + EPS) * g

    ckv_n = rms(ckv, gkva_ref[...]).astype(BF16)
    kfull = _mm(ckv_n, wk_ref[...])
    vm_ref[...] = _mm(ckv_n, wv_ref[...]).astype(BF16)
    if latent:
        qfull = _mm(rms(cq, gqa_ref[...]).astype(BF16), wq_ref[...])
        cos, slo, shi = cos_ref[...], slo_ref[...], shi_ref[...]
    gq, gk = gq_ref[...], gk_ref[...]
    for h in range(N_HEADS):
        sl = slice(h * LANES, (h + 1) * LANES)
        kh = _head_rms128(kfull[:, sl] + kr, MLA_QK) * gk
        if latent:
            kh = _rope128(kh, cos, slo, shi)
            qh = _rope128(_head_rms128(qfull[:, sl], MLA_QK) * gq, cos, slo, shi)
            qm_ref[:, sl] = qh.astype(BF16)
        km_ref[:, sl] = kh.astype(BF16)
    lane_lo = lax.broadcasted_iota(jnp.int32, (1, LANES), 1) < HEAD_DIM
    gnq, gnk = gnq_ref[...], gnk_ref[...]
    for p in range(N_HEADS // 2):
        sl = slice(p * LANES, (p + 1) * LANES)
        kn_ref[:, sl] = (_pair_rms64(nak[:, sl], lane_lo) * gnk).astype(BF16)
        if latent:
            qn_ref[:, sl] = (_pair_rms64(naq[:, sl], lane_lo) * gnq).astype(BF16)
    vn_ref[...] = nav.astype(BF16)


def attn_inputs(x2, batch_of_tile, pos_of_tile, tm, g1, sh, sc, wts, rope_tabs, *, latent):
    t, d = x2.shape
    w_in, gqa, wq, gkva, wk, wv, gq, gk, gnq, gnk = wts
    full = lambda arr: pl.BlockSpec(arr.shape, lambda i: (0,) * arr.ndim)
    row = pl.BlockSpec((None, 1, d), lambda i: (batch_of_tile(i), 0, 0))
    tab = pl.BlockSpec((tm, LANES), lambda i: (pos_of_tile(i), 0))
    wide = jax.ShapeDtypeStruct((t, N_HEADS * LANES), BF16)
    half = jax.ShapeDtypeStruct((t, N_HEADS * HEAD_DIM), BF16)
    o_wide = pl.BlockSpec((tm, N_HEADS * LANES), lambda i: (i, 0))
    o_half = pl.BlockSpec((tm, N_HEADS * HEAD_DIM), lambda i: (i, 0))
    if latent:
        out_shape, out_specs = (wide, wide, half, half, half, half), (o_wide, o_wide, o_half, o_half, o_half, o_half)
    else:
        out_shape, out_specs = (wide, half, half, half), (o_wide, o_half, o_half, o_half)
    return pl.pallas_call(
        functools.partial(_attn_in_kernel, latent=latent),
        grid=(t // tm,),
        in_specs=[pl.BlockSpec((tm, d), lambda i: (i, 0)), full(g1), row, row,
                  full(w_in), full(gqa), full(wq), full(gkva), full(wk), full(wv),
                  full(gq), full(gk), full(gnq), full(gnk), tab, tab, tab],
        out_specs=out_specs,
        out_shape=out_shape,
        compiler_params=_params("parallel"),
        name="attn_inputs_lat" if latent else "attn_inputs_ctx",
    )(x2, g1, sh, sc, w_in, gqa, wq, gkva, wk, wv, gq, gk, gnq, gnk, *rope_tabs)


def _softmax_pv(s_list, v_list):
    m = functools.reduce(jnp.maximum, [jnp.max(s, axis=-1, keepdims=True) for s in s_list])
    ps = [jnp.exp(s - m) for s in s_list]
    denom = functools.reduce(jnp.add, [jnp.sum(p, axis=-1, keepdims=True) for p in ps])
    o = functools.reduce(jnp.add, [_mm(p.astype(BF16), v) for p, v in zip(ps, v_list)])
    return o / denom


def _mla_kernel(q_ref, k_ref, v_ref, kc_ref, vc_ref, o_ref):
    lane_lo = lax.broadcasted_iota(jnp.int32, (1, LANES), 1) < HEAD_DIM
    for p in range(N_HEADS // 2):
        vsl = slice(p * LANES, (p + 1) * LANES)
        outs = []
        for h in (2 * p, 2 * p + 1):
            sl = slice(h * LANES, (h + 1) * LANES)
            q = q_ref[:, sl]
            outs.append(_softmax_pv([_mm_nt(q, k_ref[:, sl]), _mm_nt(q, kc_ref[:, sl])],
                                    [v_ref[:, vsl], vc_ref[:, vsl]]))
        o_ref[:, vsl] = jnp.where(lane_lo, outs[0], outs[1]).astype(BF16)


def mla_attention(qm, km, vm, kmc, vmc, *, tq=256):
    b, s, dq = qm.shape
    c, dv = kmc.shape[1], vm.shape[2]
    return pl.pallas_call(
        _mla_kernel,
        grid=(b, s // tq),
        in_specs=[pl.BlockSpec((None, tq, dq), lambda i, j: (i, j, 0)),
                  pl.BlockSpec((None, s, dq), lambda i, j: (i, 0, 0)),
                  pl.BlockSpec((None, s, dv), lambda i, j: (i, 0, 0)),
                  pl.BlockSpec((None, c, dq), lambda i, j: (i, 0, 0)),
                  pl.BlockSpec((None, c, dv), lambda i, j: (i, 0, 0))],
        out_specs=pl.BlockSpec((None, tq, dv), lambda i, j: (i, j, 0)),
        out_shape=jax.ShapeDtypeStruct((b, s, dv), BF16),
        compiler_params=_params("parallel", "arbitrary"),
        name="mla_attention",
    )(qm, km, vm, kmc, vmc)


def _na_row0(r, n_rows):
    return jnp.clip(r - NA_WIN_ROWS // 2, 0, n_rows - NA_WIN_ROWS)


def _na_kernel(q_ref, k_ref, v_ref, kc_ref, vc_ref, bias_ref, o_ref, *, n_rows):
    band = NA_WIN_ROWS * GRID_W
    start = pl.multiple_of(_na_row0(pl.program_id(1), n_rows) * GRID_W, GRID_W)
    lane_lo = lax.broadcasted_iota(jnp.int32, (1, LANES), 1) < HEAD_DIM
    for p in range(N_HEADS // 2):
        sl = slice(p * LANES, (p + 1) * LANES)
        qp = q_ref[:, sl]
        kp, vp = k_ref[pl.ds(start, band), sl], v_ref[pl.ds(start, band), sl]
        kcp, vcp = kc_ref[:, sl], vc_ref[:, sl]
        outs = []
        for half, keep in ((0, lane_lo), (1, jnp.logical_not(lane_lo))):
            q = jnp.where(keep, qp, jnp.zeros_like(qp))
            outs.append(_softmax_pv([_mm_nt(q, kp) + bias_ref[2 * p + half], _mm_nt(q, kcp)], [vp, vcp]))
        o_ref[:, sl] = jnp.where(lane_lo, outs[0], outs[1]).astype(BF16)


def na_attention(qn, kn, vn, knc, vnc, bias):
    b, s, dh = qn.shape
    c = knc.shape[1]
    n_rows = s // GRID_W
    band = NA_WIN_ROWS * GRID_W
    whole = lambda n: pl.BlockSpec((None, n, dh), lambda i, r: (i, 0, 0))
    return pl.pallas_call(
        functools.partial(_na_kernel, n_rows=n_rows),
        grid=(b, n_rows),
        in_specs=[pl.BlockSpec((None, GRID_W, dh), lambda i, r: (i, r, 0)),
                  whole(s), whole(s), whole(c), whole(c),
                  pl.BlockSpec((None, N_HEADS, GRID_W, band), lambda i, r: (r - _na_row0(r, n_rows), 0, 0, 0))],
        out_specs=pl.BlockSpec((None, GRID_W, dh), lambda i, r: (i, r, 0)),
        out_shape=jax.ShapeDtypeStruct((b, s, dh), BF16),
        compiler_params=_params("parallel", "arbitrary"),
        name="na_attention",
    )(qn, kn, vn, knc, vnc, bias)


def na_bias_table(rpb):
    q = np.arange(GRID_W)
    c0 = np.clip(q - NA_WIN_COLS // 2, 0, GRID_W - NA_WIN_COLS)
    kc = np.arange(GRID_W)
    inside = (kc[None, :] >= c0[:, None]) & (kc[None, :] < c0[:, None] + NA_WIN_COLS)
    rel_c = np.clip(kc[None, :] - q[:, None] + NA_WIN_COLS - 1, 0, 2 * NA_WIN_COLS - 2)
    cls = np.arange(NA_WIN_ROWS)
    rel_r = np.arange(NA_WIN_ROWS)[None, :] + NA_WIN_ROWS - 1 - cls[:, None]
    tab = rpb[:, rel_r[:, :, None, None], rel_c[None, None, :, :]]
    tab = jnp.where(inside[None, None, None], tab, NEG_BIG)
    return tab.transpose(1, 0, 3, 2, 4).reshape(NA_WIN_ROWS, rpb.shape[0], GRID_W, NA_WIN_ROWS * GRID_W)


def _out_proj_kernel(om_ref, on_ref, wa_ref, wb_ref, h_ref, gate_ref, out_ref):
    y = _mm(om_ref[...], wa_ref[...]) + _mm(on_ref[...], wb_ref[...])
    out_ref[...] = h_ref[...] + gate_ref[...] * y


def out_proj(om, on, wa, wb, h2, gate, tiles_per_batch, *, tm=512):
    t, d = h2.shape
    k = om.shape[1]
    return pl.pallas_call(
        _out_proj_kernel,
        grid=(t // tm,),
        in_specs=[pl.BlockSpec((tm, k), lambda i: (i, 0)), pl.BlockSpec((tm, k), lambda i: (i, 0)),
                  pl.BlockSpec((k, d), lambda i: (0, 0)), pl.BlockSpec((k, d), lambda i: (0, 0)),
                  pl.BlockSpec((tm, d), lambda i: (i, 0)),
                  pl.BlockSpec((None, 1, d), lambda i: (i // tiles_per_batch(tm), 0, 0))],
        out_specs=pl.BlockSpec((tm, d), lambda i: (i, 0)),
        out_shape=jax.ShapeDtypeStruct((t, d), F32),
        compiler_params=_params("parallel"),
        name="out_proj",
    )(om, on, wa, wb, h2, gate)


def _top_rows(s, k):
    rows = []
    for _ in range(k):
        m = jnp.max(s, axis=0, keepdims=True)
        rows.append(m)
        s = jnp.where(s == m, NEG_BIG, s)
    return rows


def _peer_router_kernel(x_ref, g_ref, sh_ref, sc_ref, wq_ref, keys_ref, ht_ref, sb_ref, eb_ref, th_ref, ea_ref):
    a = _rms_mod(x_ref[...], g_ref[...], sh_ref[...], sc_ref[...])
    at = a.T.astype(BF16)
    ht_ref[...] = at
    qt = _mm(wq_ref[...], at).astype(BF16)
    k = PEER_TOPK
    for h in range(PEER_HEADS):
        sa = _mm(keys_ref[2 * h], qt[(2 * h) * LANES:(2 * h + 1) * LANES])
        sb = _mm(keys_ref[2 * h + 1], qt[(2 * h + 1) * LANES:(2 * h + 2) * LANES])
        top_a, top_b = _top_rows(sa, k), _top_rows(sb, k)
        b_all = jnp.concatenate(top_b, axis=0)
        cands = [top_a[0] + b_all] + [top_a[i] + b_all[:k // 2] for i in range(1, k)]
        live = [c for c in cands]
        jr8 = lax.broadcasted_iota(jnp.int32, (k // 2, 1), 0)
        live = [live[0]] + [jnp.where(jr8 < k // (i + 1), live[i], NEG_BIG) for i in range(1, k)]
        work = list(live)
        tau = None
        for _ in range(k):
            tau = functools.reduce(jnp.maximum, [jnp.max(c, axis=0, keepdims=True) for c in work])
            work = [jnp.where(c == tau, NEG_BIG, c) for c in work]
        m = top_a[0] + top_b[0]
        z = functools.reduce(jnp.add, [jnp.sum(jnp.where(c >= tau, jnp.exp(c - m), 0.0), axis=0, keepdims=True)
                                       for c in live])
        theta = jnp.full_like(sa, -NEG_BIG)
        for j in range(k):
            theta = jnp.minimum(theta, jnp.where(sa + top_b[j] >= tau, top_b[j], -NEG_BIG))
        sb_ref[h] = sb
        eb_ref[h] = jnp.exp(sb - top_b[0]) / z
        th_ref[h] = theta
        ea_ref[h] = jnp.exp(sa - top_a[0])


def peer_router(h2, g, sh, sc, wq_t, keys, tiles_per_batch, *, tt=256):
    t, d = h2.shape
    tab = jax.ShapeDtypeStruct((PEER_HEADS, PEER_N_KEYS, t), F32)
    tab_spec = pl.BlockSpec((PEER_HEADS, PEER_N_KEYS, tt), lambda i: (0, 0, i))
    row = pl.BlockSpec((None, 1, d), lambda i: (i // tiles_per_batch(tt), 0, 0))
    return pl.pallas_call(
        _peer_router_kernel,
        grid=(t // tt,),
        in_specs=[pl.BlockSpec((tt, d), lambda i: (i, 0)), pl.BlockSpec((1, d), lambda i: (0, 0)), row, row,
                  pl.BlockSpec(wq_t.shape, lambda i: (0, 0)), pl.BlockSpec(keys.shape, lambda i: (0, 0, 0))],
        out_specs=(pl.BlockSpec((d, tt), lambda i: (0, i)), tab_spec, tab_spec, tab_spec, tab_spec),
        out_shape=(jax.ShapeDtypeStruct((d, t), BF16), tab, tab, tab, tab),
        compiler_params=_params("parallel"),
        name="peer_router",
    )(h2, g, sh, sc, wq_t, keys)


def _gelu_exact(x):
    return 0.5 * x * (1.0 + lax.erf(x * np.float32(np.sqrt(0.5))))


def _peer_expert_kernel(ht_ref, u_ref, vt_ref, sb_ref, eb_ref, th_ref, ea_ref, h_ref, gate_ref, out_ref,
                        acc_ref, w_ref, *, rows_per_block):
    e = pl.program_id(1)

    @pl.when(e == 0)
    def _():
        acc_ref[...] = jnp.zeros_like(acc_ref)

    act = _gelu_exact(_mm(u_ref[...], ht_ref[...]))
    for il in range(rows_per_block):
        g = None
        for h in range(PEER_HEADS):
            term = jnp.where(sb_ref[h] >= th_ref[h, il:il + 1, :], eb_ref[h], 0.0) * ea_ref[h, il:il + 1, :]
            g = term if g is None else g + term
        sl = slice(il * PEER_N_KEYS, (il + 1) * PEER_N_KEYS)
        w_ref[sl, :] = (g * act[sl, :]).astype(BF16)
    acc_ref[...] += _mm(vt_ref[...], w_ref[...])

    @pl.when(e == pl.num_programs(1) - 1)
    def _():
        out_ref[...] = h_ref[...] + gate_ref[...] * acc_ref[...].T


def peer_experts(ht, u_bf, vt_bf, sb, eb, th, ea, h2, gate, tiles_per_batch, *, tt=256, te=1024):
    t, d = h2.shape
    n_exp = u_bf.shape[0]
    rows = te // PEER_N_KEYS
    full_tab = pl.BlockSpec((PEER_HEADS, PEER_N_KEYS, tt), lambda i, e: (0, 0, i))
    row_tab = pl.BlockSpec((PEER_HEADS, rows, tt), lambda i, e: (0, e, i))
    return pl.pallas_call(
        functools.partial(_peer_expert_kernel, rows_per_block=rows),
        grid=(t // tt, n_exp // te),
        in_specs=[pl.BlockSpec((d, tt), lambda i, e: (0, i)),
                  pl.BlockSpec((te, d), lambda i, e: (e, 0)),
                  pl.BlockSpec((d, te), lambda i, e: (0, e)),
                  full_tab, full_tab, row_tab, row_tab,
                  pl.BlockSpec((tt, d), lambda i, e: (i, 0)),
                  pl.BlockSpec((None, 1, d), lambda i, e: (i // tiles_per_batch(tt), 0, 0))],
        out_specs=pl.BlockSpec((tt, d), lambda i, e: (i, 0)),
        out_shape=jax.ShapeDtypeStruct((t, d), F32),
        scratch_shapes=[pltpu.VMEM((d, tt), F32), pltpu.VMEM((te, tt), BF16)],
        compiler_params=_params("parallel", "arbitrary"),
        name="peer_experts",
    )(ht, u_bf, vt_bf, sb, eb, th, ea, h2, gate)


def peer_layer(h2, g, sh, sc, gate, w_query, sub_keys, u_tab, v_tab, tiles_per_batch):
    wq_t = w_query.T.astype(BF16)
    keys = sub_keys.reshape(PEER_HEADS * 2, PEER_N_KEYS, -1).astype(BF16)
    ht, sb, eb, th, ea = peer_router(h2, g, sh, sc, wq_t, keys, tiles_per_batch)
    return peer_experts(ht, u_tab.astype(BF16), v_tab.T.astype(BF16), sb, eb, th, ea, h2, gate, tiles_per_batch)


def _glu_kernel(x_ref, g_ref, sh_ref, sc_ref, w_ref, b_ref, out_ref):
    a = _rms_mod(x_ref[...], g_ref[...], sh_ref[...], sc_ref[...]).astype(BF16)
    y = _mm(a, w_ref[...]) + b_ref[...]
    d = out_ref.shape[-1]
    out_ref[...] = y[:, :d] * jax.nn.sigmoid(y[:, d:])


def conv_glu(h2, g, sh, sc, w_pw1, b_pw1, tiles_per_batch, *, tm=512):
    t, d = h2.shape
    row = pl.BlockSpec((None, 1, d), lambda i: (i // tiles_per_batch(tm), 0, 0))
    return pl.pallas_call(
        _glu_kernel,
        grid=(t // tm,),
        in_specs=[pl.BlockSpec((tm, d), lambda i: (i, 0)), pl.BlockSpec((1, d), lambda i: (0, 0)), row, row,
                  pl.BlockSpec((d, 2 * d), lambda i: (0, 0)), pl.BlockSpec((1, 2 * d), lambda i: (0, 0))],
        out_specs=pl.BlockSpec((tm, d), lambda i: (i, 0)),
        out_shape=jax.ShapeDtypeStruct((t, d), F32),
        compiler_params=_params("parallel"),
        name="conv_glu",
    )(h2, g, sh, sc, w_pw1, b_pw1)


def _conv_tail_kernel(prev_ref, cur_ref, next_ref, wdw_ref, bdw_ref, gln_ref, bln_ref, w2_ref, b2_ref,
                      h_ref, gate_ref, out_ref, win_ref, *, ts):
    s = pl.program_id(1)
    halo = CONV_HALO
    win_ref[0:halo, :] = jnp.where(s > 0, prev_ref[...], 0.0)
    win_ref[halo:halo + ts, :] = cur_ref[...]
    win_ref[halo + ts:, :] = jnp.where(s < pl.num_programs(1) - 1, next_ref[...], 0.0)
    acc = jnp.zeros((ts, cur_ref.shape[-1]), F32) + bdw_ref[...]
    base = halo - CONV_WIDTH // 2
    for k in range(CONV_WIDTH):
        acc = acc + win_ref[base + k:base + k + ts, :] * wdw_ref[k:k + 1, :]
    mu = jnp.mean(acc, axis=-1, keepdims=True)
    cen = acc - mu
    var = jnp.mean(cen * cen, axis=-1, keepdims=True)
    y = cen * lax.rsqrt(var + EPS) * gln_ref[...] + bln_ref[...]
    y = y * jax.nn.sigmoid(y)
    y = _mm(y.astype(BF16), w2_ref[...]) + b2_ref[...]
    out_ref[...] = h_ref[...] + gate_ref[...] * y


def conv_tail(a3, w_dw, b_dw, g_ln, b_ln, w_pw2, b_pw2, h3, gate, *, ts=256):
    b, s, d = a3.shape
    hb = ts // CONV_HALO
    n_halo = s // CONV_HALO
    vec = pl.BlockSpec((1, d), lambda i, j: (0, 0))
    return pl.pallas_call(
        functools.partial(_conv_tail_kernel, ts=ts),
        grid=(b, s // ts),
        in_specs=[pl.BlockSpec((None, CONV_HALO, d), lambda i, j: (i, jnp.maximum(j * hb - 1, 0), 0)),
                  pl.BlockSpec((None, ts, d), lambda i, j: (i, j, 0)),
                  pl.BlockSpec((None, CONV_HALO, d), lambda i, j: (i, jnp.minimum((j + 1) * hb, n_halo - 1), 0)),
                  pl.BlockSpec((CONV_WIDTH, d), lambda i, j: (0, 0)), vec, vec, vec,
                  pl.BlockSpec((d, d), lambda i, j: (0, 0)), vec,
                  pl.BlockSpec((None, ts, d), lambda i, j: (i, j, 0)),
                  pl.BlockSpec((None, 1, d), lambda i, j: (i, 0, 0))],
        out_specs=pl.BlockSpec((None, ts, d), lambda i, j: (i, j, 0)),
        out_shape=jax.ShapeDtypeStruct((b, s, d), F32),
        scratch_shapes=[pltpu.VMEM((ts + 2 * CONV_HALO, d), F32)],
        compiler_params=_params("parallel", "arbitrary"),
        name="conv_tail",
    )(a3, a3, a3, w_dw, b_dw, g_ln, b_ln, w_pw2, b_pw2, h3, gate)


def _pad_heads(w, width):
    k = w.shape[0]
    w = w.reshape(k, N_HEADS, width)
    return jnp.pad(w, ((0, 0), (0, 0), (0, LANES - width))).reshape(k, N_HEADS * LANES)


def _attn_weights(w_in, g_qa, w_q_up, g_kva, w_kv_up, g_q, g_k, na_g_q, na_g_k):
    d = w_in.shape[0]
    a, b, c = 384, 384 + 256, 384 + 256 + MLA_ROPE
    kr_cols = jnp.pad(w_in[:, b:c], ((0, 0), (MLA_NOPE, LANES - MLA_QK)))
    w_in_r = jnp.concatenate([w_in[:, :a], w_in[:, a:b], kr_cols, w_in[:, c:]], axis=1).astype(BF16)
    wq = _pad_heads(w_q_up, MLA_QK).astype(BF16)
    kv = w_kv_up.reshape(-1, N_HEADS, MLA_NOPE + HEAD_DIM)
    wk = _pad_heads(kv[:, :, :MLA_NOPE].reshape(-1, N_HEADS * MLA_NOPE), MLA_NOPE).astype(BF16)
    wv = kv[:, :, MLA_NOPE:].reshape(-1, N_HEADS * HEAD_DIM).astype(BF16)
    pad_g = lambda g: jnp.pad(g, (0, LANES - MLA_QK)).reshape(1, LANES)
    gq = pad_g(g_q) * (MLA_QK ** -0.5)
    gk = pad_g(g_k)
    gnq = jnp.tile(na_g_q, 2).reshape(1, LANES) * (HEAD_DIM ** -0.5)
    gnk = jnp.tile(na_g_k, 2).reshape(1, LANES)
    del d
    return (w_in_r, g_qa.reshape(1, -1), wq, g_kva.reshape(1, -1), wk, wv, gq, gk, gnq, gnk)


def _rope_tables(s):
    t = jnp.arange(s)
    row = (t // GRID_W).astype(F32)
    col = (t % GRID_W).astype(F32)
    n_freq = MLA_ROPE // 4
    inv_freq = 1.0 / (ROPE_THETA ** (jnp.arange(n_freq, dtype=F32) / n_freq))
    ang = jnp.concatenate([row[:, None] * inv_freq, col[:, None] * inv_freq], axis=-1)
    cos, sin = jnp.cos(ang), jnp.sin(ang)
    half = MLA_ROPE // 2
    ones = jnp.ones((s, MLA_NOPE), F32)
    z = lambda n: jnp.zeros((s, n), F32)
    cos_t = jnp.concatenate([ones, cos, cos, z(LANES - MLA_QK)], axis=1)
    sin_lo = jnp.concatenate([z(MLA_NOPE), -sin, z(half), z(LANES - MLA_QK)], axis=1)
    sin_hi = jnp.concatenate([z(MLA_NOPE), z(half), sin, z(LANES - MLA_QK)], axis=1)
    return cos_t, sin_lo, sin_hi


def kernel(x, c, ctx, c_ctx, w_ada, b_ada, g_norm1, g_norm2, attn_w_in, mla_g_qa, mla_w_q_up, mla_g_kva, mla_w_kv_up, mla_g_q, mla_g_k, na_g_q, na_g_k, na_rpb, attn_w_out, conv_w_pw1, conv_b_pw1, conv_w_dw, conv_b_dw, conv_g_ln, conv_b_ln, conv_w_pw2, conv_b_pw2, peer_w_query, peer_sub_keys, peer_u, peer_v):
    b, s, d = x.shape
    n_ctx = ctx.shape[1]
    t = b * s
    depth = w_ada.shape[0]
    tiles_per_batch = lambda tile: s // tile

    cond = jnp.zeros((16, d), F32).at[:b].set(c).at[b].set(c_ctx)
    mod = ada_params(cond, w_ada, b_ada).reshape(depth, 16, 6, 1, d)
    mods = [[mod[l, :, k] for k in range(6)] for l in range(depth)]

    h2 = x.reshape(t, d)
    for l in range(depth):
        sh1, sc1, g1, sh2, sc2, g2 = mods[l]
        i = l // 2
        gn1 = g_norm1[l].reshape(1, d)
        if l % 2 == 0:
            wts = _attn_weights(attn_w_in[i], mla_g_qa[i], mla_w_q_up[i], mla_g_kva[i], mla_w_kv_up[i],
                                mla_g_q[i], mla_g_k[i], na_g_q[i], na_g_k[i])
            tabs = _rope_tables(s)
            tm = 512
            qm, km, vm, qn, kn, vn = attn_inputs(
                h2, lambda j: j // (s // tm), lambda j: j % (s // tm), tm, gn1, sh1, sc1, wts, tabs, latent=True)
            tc = n_ctx
            kmc, vmc, knc, vnc = attn_inputs(
                ctx.reshape(b * n_ctx, d), lambda j: b, lambda j: 0, tc, gn1, sh1, sc1, wts,
                tuple(tb[:tc] for tb in tabs), latent=False)
            r3 = lambda arr, n: arr.reshape(b, n, arr.shape[-1])
            om = mla_attention(r3(qm, s), r3(km, s), r3(vm, s), r3(kmc, n_ctx), r3(vmc, n_ctx))
            on = na_attention(r3(qn, s), r3(kn, s), r3(vn, s), r3(knc, n_ctx), r3(vnc, n_ctx),
                              na_bias_table(na_rpb[i]))
            w_out = attn_w_out[i].astype(BF16)
            half = w_out.shape[0] // 2
            h2 = out_proj(om.reshape(t, -1), on.reshape(t, -1), w_out[:half], w_out[half:], h2, g1,
                          tiles_per_batch)
        else:
            a2 = conv_glu(h2, gn1, sh1, sc1, conv_w_pw1[i].astype(BF16), conv_b_pw1[i].reshape(1, -1),
                          tiles_per_batch)
            h2 = conv_tail(a2.reshape(b, s, d), conv_w_dw[i], conv_b_dw[i].reshape(1, d),
                           conv_g_ln[i].reshape(1, d), conv_b_ln[i].reshape(1, d),
                           conv_w_pw2[i].astype(BF16), conv_b_pw2[i].reshape(1, d),
                           h2.reshape(b, s, d), g1).reshape(t, d)
        h2 = peer_layer(h2, g_norm2[l].reshape(1, d), sh2, sc2, g2, peer_w_query[l], peer_sub_keys[l],
                        peer_u[l], peer_v[l], tiles_per_batch)
    return h2.reshape(b, s, d)
```

```python
import functools

import jax
import jax.numpy as jnp
import numpy as np
from jax import lax
from jax.experimental import pallas as pl
from jax.experimental.pallas import tpu as pltpu

F32 = jnp.float32
BF16 = jnp.bfloat16

LANES = 128
EPS = 1e-6
NEG_BIG = -1e30
VMEM_LIMIT = 56 * 1024 * 1024

GRID_W = 64
HEAD_DIM = 64
N_HEADS = 8
MLA_NOPE = 64
MLA_ROPE = 32
MLA_QK = MLA_NOPE + MLA_ROPE
NA_WIN_ROWS = 8
NA_WIN_COLS = 16
ROPE_THETA = 10000.0
CONV_WIDTH = 31
CONV_HALO = 16
PEER_HEADS = 8
PEER_N_KEYS = 128
PEER_TOPK = 16

NT_DIMS = (((1,), (1,)), ((), ()))


def _params(*sem):
    return pltpu.CompilerParams(dimension_semantics=sem, vmem_limit_bytes=VMEM_LIMIT)


def _mm(a, b):
    return jnp.dot(a, b, preferred_element_type=F32)


def _mm_nt(a, b):
    return lax.dot_general(a, b, NT_DIMS, preferred_element_type=F32)


def _rms_mod(x, g, shift, scale):
    inv = lax.rsqrt(jnp.mean(x * x, axis=-1, keepdims=True) + EPS)
    return (x * inv * g) * (1.0 + scale) + shift


def _ada_kernel(cond_ref, w_ref, b_ref, out_ref):
    cond = cond_ref[...]
    a = cond * jax.nn.sigmoid(cond)
    out_ref[...] = jnp.dot(a, w_ref[...], preferred_element_type=F32,
                           precision=lax.Precision.HIGHEST) + b_ref[...]


def ada_params(cond, w_ada, b_ada, *, tn=1536):
    depth, d, n = w_ada.shape
    rows = cond.shape[0]
    return pl.pallas_call(
        _ada_kernel,
        grid=(depth, n // tn),
        in_specs=[pl.BlockSpec((rows, d), lambda l, j: (0, 0)),
                  pl.BlockSpec((None, d, tn), lambda l, j: (l, 0, j)),
                  pl.BlockSpec((None, 1, tn), lambda l, j: (l, 0, j))],
        out_specs=pl.BlockSpec((None, rows, tn), lambda l, j: (l, 0, j)),
        out_shape=jax.ShapeDtypeStruct((depth, rows, n), F32),
        compiler_params=_params("parallel", "parallel"),
        name="ada_params",
    )(cond, w_ada, b_ada.reshape(depth, 1, n))


def _rope128(x, cos, sin_lo, sin_hi):
    return x * cos + pltpu.roll(x, LANES - MLA_ROPE // 2, 1) * sin_lo + pltpu.roll(x, MLA_ROPE // 2, 1) * sin_hi


def _head_rms128(x, n_valid):
    ss = jnp.sum(x * x, axis=-1, keepdims=True) * (1.0 / n_valid)
    return x * lax.rsqrt(ss + EPS)


def _pair_rms64(x, lane_lo):
    xx = x * x
    ss_lo = jnp.sum(jnp.where(lane_lo, xx, 0.0), axis=-1, keepdims=True)
    ss_hi = jnp.sum(jnp.where(lane_lo, 0.0, xx), axis=-1, keepdims=True)
    inv = jnp.where(lane_lo, lax.rsqrt(ss_lo * (1.0 / HEAD_DIM) + EPS), lax.rsqrt(ss_hi * (1.0 / HEAD_DIM) + EPS))
    return x * inv


def _attn_in_kernel(x_ref, g1_ref, sh_ref, sc_ref, w_in_ref, gqa_ref, wq_ref, gkva_ref, wk_ref, wv_ref,
                    gq_ref, gk_ref, gnq_ref, gnk_ref, cos_ref, slo_ref, shi_ref, *out_refs, latent):
    if latent:
        qm_ref, km_ref, vm_ref, qn_ref, kn_ref, vn_ref = out_refs
    else:
        km_ref, vm_ref, kn_ref, vn_ref = out_refs
    a = _rms_mod(x_ref[...], g1_ref[...], sh_ref[...], sc_ref[...]).astype(BF16)
    z = _mm(a, w_in_ref[...])
    c0, c1, c2 = 384, 384 + 256, 384 + 256 + LANES
    cq, ckv, kr = z[:, :c0], z[:, c0:c1], z[:, c1:c2]
    naq, nak, nav = z[:, c2:c2 + 512], z[:, c2 + 512:c2 + 1024], z[:, c2 + 1024:c2 + 1536]

    def rms(v, g):
        return v * lax.rsqrt(jnp.mean(v * v, axis=-1, keepdims=True) + EPS) * g

    ckv_n = rms(ckv, gkva_ref[...]).astype(BF16)
    kfull = _mm(ckv_n, wk_ref[...])
    vm_ref[...] = _mm(ckv_n, wv_ref[...]).astype(BF16)
    if latent:
        qfull = _mm(rms(cq, gqa_ref[...]).astype(BF16), wq_ref[...])
        cos, slo, shi = cos_ref[...], slo_ref[...], shi_ref[...]
    gq, gk = gq_ref[...], gk_ref[...]
    for h in range(N_HEADS):
        sl = slice(h * LANES, (h + 1) * LANES)
        kh = _head_rms128(kfull[:, sl] + kr, MLA_QK) * gk
        if latent:
            kh = _rope128(kh, cos, slo, shi)
            qh = _rope128(_head_rms128(qfull[:, sl], MLA_QK) * gq, cos, slo, shi)
            qm_ref[:, sl] = qh.astype(BF16)
        km_ref[:, sl] = kh.astype(BF16)
    lane_lo = lax.broadcasted_iota(jnp.int32, (1, LANES), 1) < HEAD_DIM
    gnq, gnk = gnq_ref[...], gnk_ref[...]
    for p in range(N_HEADS // 2):
        sl = slice(p * LANES, (p + 1) * LANES)
        kn_ref[:, sl] = (_pair_rms64(nak[:, sl], lane_lo) * gnk).astype(BF16)
        if latent:
            qn_ref[:, sl] = (_pair_rms64(naq[:, sl], lane_lo) * gnq).astype(BF16)
    vn_ref[...] = nav.astype(BF16)


def attn_inputs(x2, batch_of_tile, pos_of_tile, tm, g1, sh, sc, wts, rope_tabs, *, latent):
    t, d = x2.shape
    w_in, gqa, wq, gkva, wk, wv, gq, gk, gnq, gnk = wts
    full = lambda arr: pl.BlockSpec(arr.shape, lambda i: (0,) * arr.ndim)
    row = pl.BlockSpec((None, 1, d), lambda i: (batch_of_tile(i), 0, 0))
    tab = pl.BlockSpec((tm, LANES), lambda i: (pos_of_tile(i), 0))
    wide = jax.ShapeDtypeStruct((t, N_HEADS * LANES), BF16)
    half = jax.ShapeDtypeStruct((t, N_HEADS * HEAD_DIM), BF16)
    o_wide = pl.BlockSpec((tm, N_HEADS * LANES), lambda i: (i, 0))
    o_half = pl.BlockSpec((tm, N_HEADS * HEAD_DIM), lambda i: (i, 0))
    if latent:
        out_shape, out_specs = (wide, wide, half, half, half, half), (o_wide, o_wide, o_half, o_half, o_half, o_half)
    else:
        out_shape, out_specs = (wide, half, half, half), (o_wide, o_half, o_half, o_half)
    return pl.pallas_call(
        functools.partial(_attn_in_kernel, latent=latent),
        grid=(t // tm,),
        in_specs=[pl.BlockSpec((tm, d), lambda i: (i, 0)), full(g1), row, row,
                  full(w_in), full(gqa), full(wq), full(gkva), full(wk), full(wv),
                  full(gq), full(gk), full(gnq), full(gnk), tab, tab, tab],
        out_specs=out_specs,
        out_shape=out_shape,
        compiler_params=_params("parallel"),
        name="attn_inputs_lat" if latent else "attn_inputs_ctx",
    )(x2, g1, sh, sc, w_in, gqa, wq, gkva, wk, wv, gq, gk, gnq, gnk, *rope_tabs)


def _softmax_pv(s_list, v_list):
    m = functools.reduce(jnp.maximum, [jnp.max(s, axis=-1, keepdims=True) for s in s_list])
    ps = [jnp.exp(s - m) for s in s_list]
    denom = functools.reduce(jnp.add, [jnp.sum(p, axis=-1, keepdims=True) for p in ps])
    o = functools.reduce(jnp.add, [_mm(p.astype(BF16), v) for p, v in zip(ps, v_list)])
    return o / denom


def _mla_kernel(q_ref, k_ref, v_ref, kc_ref, vc_ref, o_ref):
    lane_lo = lax.broadcasted_iota(jnp.int32, (1, LANES), 1) < HEAD_DIM
    for p in range(N_HEADS // 2):
        vsl = slice(p * LANES, (p + 1) * LANES)
        outs = []
        for h in (2 * p, 2 * p + 1):
            sl = slice(h * LANES, (h + 1) * LANES)
            q = q_ref[:, sl]
            outs.append(_softmax_pv([_mm_nt(q, k_ref[:, sl]), _mm_nt(q, kc_ref[:, sl])],
                                    [v_ref[:, vsl], vc_ref[:, vsl]]))
        o_ref[:, vsl] = jnp.where(lane_lo, outs[0], outs[1]).astype(BF16)


def mla_attention(qm, km, vm, kmc, vmc, *, tq=256):
    b, s, dq = qm.shape
    c, dv = kmc.shape[1], vm.shape[2]
    return pl.pallas_call(
        _mla_kernel,
        grid=(b, s // tq),
        in_specs=[pl.BlockSpec((None, tq, dq), lambda i, j: (i, j, 0)),
                  pl.BlockSpec((None, s, dq), lambda i, j: (i, 0, 0)),
                  pl.BlockSpec((None, s, dv), lambda i, j: (i, 0, 0)),
                  pl.BlockSpec((None, c, dq), lambda i, j: (i, 0, 0)),
                  pl.BlockSpec((None, c, dv), lambda i, j: (i, 0, 0))],
        out_specs=pl.BlockSpec((None, tq, dv), lambda i, j: (i, j, 0)),
        out_shape=jax.ShapeDtypeStruct((b, s, dv), BF16),
        compiler_params=_params("parallel", "arbitrary"),
        name="mla_attention",
    )(qm, km, vm, kmc, vmc)


def _na_row0(r, n_rows):
    return jnp.clip(r - NA_WIN_ROWS // 2, 0, n_rows - NA_WIN_ROWS)


def _na_kernel(q_ref, k_ref, v_ref, kc_ref, vc_ref, bias_ref, o_ref, *, n_rows):
    band = NA_WIN_ROWS * GRID_W
    start = pl.multiple_of(_na_row0(pl.program_id(1), n_rows) * GRID_W, GRID_W)
    lane_lo = lax.broadcasted_iota(jnp.int32, (1, LANES), 1) < HEAD_DIM
    for p in range(N_HEADS // 2):
        sl = slice(p * LANES, (p + 1) * LANES)
        qp = q_ref[:, sl]
        kp, vp = k_ref[pl.ds(start, band), sl], v_ref[pl.ds(start, band), sl]
        kcp, vcp = kc_ref[:, sl], vc_ref[:, sl]
        outs = []
        for half, keep in ((0, lane_lo), (1, jnp.logical_not(lane_lo))):
            q = jnp.where(keep, qp, jnp.zeros_like(qp))
            outs.append(_softmax_pv([_mm_nt(q, kp) + bias_ref[2 * p + half], _mm_nt(q, kcp)], [vp, vcp]))
        o_ref[:, sl] = jnp.where(lane_lo, outs[0], outs[1]).astype(BF16)


def na_attention(qn, kn, vn, knc, vnc, bias):
    b, s, dh = qn.shape
    c = knc.shape[1]
    n_rows = s // GRID_W
    band = NA_WIN_ROWS * GRID_W
    whole = lambda n: pl.BlockSpec((None, n, dh), lambda i, r: (i, 0, 0))
    return pl.pallas_call(
        functools.partial(_na_kernel, n_rows=n_rows),
        grid=(b, n_rows),
        in_specs=[pl.BlockSpec((None, GRID_W, dh), lambda i, r: (i, r, 0)),
                  whole(s), whole(s), whole(c), whole(c),
                  pl.BlockSpec((None, N_HEADS, GRID_W, band), lambda i, r: (r - _na_row0(r, n_rows), 0, 0, 0))],
        out_specs=pl.BlockSpec((None, GRID_W, dh), lambda i, r: (i, r, 0)),
        out_shape=jax.ShapeDtypeStruct((b, s, dh), BF16),
        compiler_params=_params("parallel", "arbitrary"),
        name="na_attention",
    )(qn, kn, vn, knc, vnc, bias)


def na_bias_table(rpb):
    n_heads, n_rel_r, n_rel_c = rpb.shape
    q = np.arange(GRID_W)
    c0 = np.clip(q - NA_WIN_COLS // 2, 0, GRID_W - NA_WIN_COLS)
    kc = np.arange(GRID_W)
    inside = (kc[None, :] >= c0[:, None]) & (kc[None, :] < c0[:, None] + NA_WIN_COLS)
    rel_c = kc[None, :] - q[:, None] + NA_WIN_COLS - 1
    onehot = ((rel_c[None] == np.arange(n_rel_c)[:, None, None]) & inside[None]).astype(np.float32)
    toep = jnp.einsum("hrc,cqk->hrqk", rpb, jnp.asarray(onehot), precision=lax.Precision.HIGHEST)
    toep = jnp.where(inside[None, None], toep, NEG_BIG)
    per_cls = [toep[:, NA_WIN_ROWS - 1 - cls:2 * NA_WIN_ROWS - 1 - cls] for cls in range(NA_WIN_ROWS)]
    tab = jnp.stack(per_cls)
    return tab.transpose(0, 1, 3, 2, 4).reshape(NA_WIN_ROWS, n_heads, GRID_W, NA_WIN_ROWS * GRID_W)


def _out_proj_kernel(om_ref, on_ref, wa_ref, wb_ref, h_ref, gate_ref, out_ref):
    y = _mm(om_ref[...], wa_ref[...]) + _mm(on_ref[...], wb_ref[...])
    out_ref[...] = h_ref[...] + gate_ref[...] * y


def out_proj(om, on, wa, wb, h2, gate, tiles_per_batch, *, tm=512):
    t, d = h2.shape
    k = om.shape[1]
    return pl.pallas_call(
        _out_proj_kernel,
        grid=(t // tm,),
        in_specs=[pl.BlockSpec((tm, k), lambda i: (i, 0)), pl.BlockSpec((tm, k), lambda i: (i, 0)),
                  pl.BlockSpec((k, d), lambda i: (0, 0)), pl.BlockSpec((k, d), lambda i: (0, 0)),
                  pl.BlockSpec((tm, d), lambda i: (i, 0)),
                  pl.BlockSpec((None, 1, d), lambda i: (i // tiles_per_batch(tm), 0, 0))],
        out_specs=pl.BlockSpec((tm, d), lambda i: (i, 0)),
        out_shape=jax.ShapeDtypeStruct((t, d), F32),
        compiler_params=_params("parallel"),
        name="out_proj",
    )(om, on, wa, wb, h2, gate)


def _top_rows(s, k):
    rows = []
    for _ in range(k):
        m = jnp.max(s, axis=0, keepdims=True)
        rows.append(m)
        s = jnp.where(s == m, NEG_BIG, s)
    return rows


def _peer_router_kernel(x_ref, g_ref, sh_ref, sc_ref, wq_ref, keys_ref, ht_ref, sb_ref, eb_ref, th_ref, ea_ref):
    a = _rms_mod(x_ref[...], g_ref[...], sh_ref[...], sc_ref[...])
    at = a.T.astype(BF16)
    ht_ref[...] = at
    qt = _mm(wq_ref[...], at).astype(BF16)
    k = PEER_TOPK
    for h in range(PEER_HEADS):
        sa = _mm(keys_ref[2 * h], qt[(2 * h) * LANES:(2 * h + 1) * LANES])
        sb = _mm(keys_ref[2 * h + 1], qt[(2 * h + 1) * LANES:(2 * h + 2) * LANES])
        top_a, top_b = _top_rows(sa, k), _top_rows(sb, k)
        b_all = jnp.concatenate(top_b, axis=0)
        b_lo, b_hi = b_all[:k // 2], b_all[k // 2:]
        jr8 = lax.broadcasted_iota(jnp.int32, (k // 2, 1), 0)
        live = [top_a[0] + b_lo, top_a[0] + b_hi]
        live += [jnp.where(jr8 < k // (i + 1), top_a[i] + b_lo, NEG_BIG) for i in range(1, k)]
        work = list(live)
        tau = None
        for _ in range(k):
            tau = jnp.max(functools.reduce(jnp.maximum, work), axis=0, keepdims=True)
            work = [jnp.where(c == tau, NEG_BIG, c) for c in work]
        m = top_a[0] + top_b[0]
        z = functools.reduce(jnp.add, [jnp.sum(jnp.where(c >= tau, jnp.exp(c - m), 0.0), axis=0, keepdims=True)
                                       for c in live])
        theta = jnp.full_like(sa, -NEG_BIG)
        for j in range(k):
            theta = jnp.minimum(theta, jnp.where(sa + top_b[j] >= tau, top_b[j], -NEG_BIG))
        sb_ref[h] = sb
        eb_ref[h] = jnp.exp(sb - top_b[0]) / z
        th_ref[h] = theta
        ea_ref[h] = jnp.exp(sa - top_a[0])


def peer_router(h2, g, sh, sc, wq_t, keys, tiles_per_batch, *, tt=256):
    t, d = h2.shape
    tab = jax.ShapeDtypeStruct((PEER_HEADS, PEER_N_KEYS, t), F32)
    tab_spec = pl.BlockSpec((PEER_HEADS, PEER_N_KEYS, tt), lambda i: (0, 0, i))
    row = pl.BlockSpec((None, 1, d), lambda i: (i // tiles_per_batch(tt), 0, 0))
    return pl.pallas_call(
        _peer_router_kernel,
        grid=(t // tt,),
        in_specs=[pl.BlockSpec((tt, d), lambda i: (i, 0)), pl.BlockSpec((1, d), lambda i: (0, 0)), row, row,
                  pl.BlockSpec(wq_t.shape, lambda i: (0, 0)), pl.BlockSpec(keys.shape, lambda i: (0, 0, 0))],
        out_specs=(pl.BlockSpec((d, tt), lambda i: (0, i)), tab_spec, tab_spec, tab_spec, tab_spec),
        out_shape=(jax.ShapeDtypeStruct((d, t), BF16), tab, tab, tab, tab),
        compiler_params=_params("parallel"),
        name="peer_router",
    )(h2, g, sh, sc, wq_t, keys)


def _gelu_exact(x):
    return 0.5 * x * (1.0 + lax.erf(x * np.float32(np.sqrt(0.5))))


def _peer_expert_kernel(ht_ref, u_ref, vt_ref, sb_ref, eb_ref, th_ref, ea_ref, h_ref, gate_ref, out_ref,
                        acc_ref, w_even_ref, w_odd_ref, *, rows_per_block, n_blocks):
    e = pl.program_id(1)
    tt = acc_ref.shape[1]

    d = acc_ref.shape[0]
    n_chunks = rows_per_block // 2
    rows_c = 2 * PEER_N_KEYS
    d_c = d // n_chunks
    jr = 32

    def apply_chunk(w_prev, c):
        rows = slice(c * d_c, (c + 1) * d_c)
        acc_ref[rows, :] += _mm(vt_ref[rows, :], w_prev[...])

    def act_chunk(c):
        rows = slice(c * rows_c, (c + 1) * rows_c)
        return _gelu_exact(_mm(u_ref[rows, :], ht_ref[...]))

    def gate_chunk(w_next, act, c):
        ils = (2 * c, 2 * c + 1)
        for lg in range(tt // LANES):
            lanes = slice(lg * LANES, (lg + 1) * LANES)
            for jc in range(PEER_N_KEYS // jr):
                js = slice(jc * jr, (jc + 1) * jr)
                g = [None] * len(ils)
                for h in range(PEER_HEADS):
                    sb, eb = sb_ref[h, js, lanes], eb_ref[h, js, lanes]
                    for n, il in enumerate(ils):
                        term = jnp.where(sb >= th_ref[h, il:il + 1, lanes], eb, 0.0) * ea_ref[h, il:il + 1, lanes]
                        g[n] = term if g[n] is None else g[n] + term
                for n, il in enumerate(ils):
                    local = slice(n * PEER_N_KEYS + jc * jr, n * PEER_N_KEYS + (jc + 1) * jr)
                    rows = slice(il * PEER_N_KEYS + jc * jr, il * PEER_N_KEYS + (jc + 1) * jr)
                    w_next[rows, lanes] = (g[n] * act[local, lanes]).astype(BF16)

    def step(w_prev, w_next):
        act = act_chunk(0) if w_next is not None else None
        for c in range(n_chunks):
            nxt = act_chunk(c + 1) if (w_next is not None and c + 1 < n_chunks) else None
            if w_prev is not None:
                apply_chunk(w_prev, c)
            if w_next is not None:
                gate_chunk(w_next, act, c)
            act = nxt

    @pl.when(e == 0)
    def _():
        acc_ref[...] = jnp.zeros_like(acc_ref)
        step(None, w_even_ref)

    middle = jnp.logical_and(e > 0, e < n_blocks)

    @pl.when(jnp.logical_and(middle, e % 2 == 1))
    def _():
        step(w_even_ref, w_odd_ref)

    @pl.when(jnp.logical_and(middle, e % 2 == 0))
    def _():
        step(w_odd_ref, w_even_ref)

    @pl.when(e == n_blocks)
    def _():
        step(w_even_ref if (n_blocks - 1) % 2 == 0 else w_odd_ref, None)
        out_ref[...] = h_ref[...] + gate_ref[...] * acc_ref[...].T


def peer_experts(ht, u_bf, vt_bf, sb, eb, th, ea, h2, gate, tiles_per_batch, *, tt=512, te=1024):
    t, d = h2.shape
    n_blocks = u_bf.shape[0] // te
    rows = te // PEER_N_KEYS
    last = n_blocks - 1
    full_tab = pl.BlockSpec((PEER_HEADS, PEER_N_KEYS, tt), lambda i, e: (0, 0, i))
    row_tab = pl.BlockSpec((PEER_HEADS, rows, tt), lambda i, e: (0, jnp.minimum(e, last), i))
    return pl.pallas_call(
        functools.partial(_peer_expert_kernel, rows_per_block=rows, n_blocks=n_blocks),
        grid=(t // tt, n_blocks + 1),
        in_specs=[pl.BlockSpec((d, tt), lambda i, e: (0, i)),
                  pl.BlockSpec((te, d), lambda i, e: (jnp.minimum(e, last), 0)),
                  pl.BlockSpec((d, te), lambda i, e: (0, jnp.maximum(e - 1, 0))),
                  full_tab, full_tab, row_tab, row_tab,
                  pl.BlockSpec((tt, d), lambda i, e: (i, 0)),
                  pl.BlockSpec((None, 1, d), lambda i, e: (i // tiles_per_batch(tt), 0, 0))],
        out_specs=pl.BlockSpec((tt, d), lambda i, e: (i, 0)),
        out_shape=jax.ShapeDtypeStruct((t, d), F32),
        scratch_shapes=[pltpu.VMEM((d, tt), F32), pltpu.VMEM((te, tt), BF16), pltpu.VMEM((te, tt), BF16)],
        compiler_params=_params("parallel", "arbitrary"),
        name="peer_experts",
    )(ht, u_bf, vt_bf, sb, eb, th, ea, h2, gate)


def peer_layer(h2, g, sh, sc, gate, w_query, sub_keys, u_tab, v_tab, tiles_per_batch):
    wq_t = w_query.T.astype(BF16)
    keys = sub_keys.reshape(PEER_HEADS * 2, PEER_N_KEYS, -1).astype(BF16)
    ht, sb, eb, th, ea = peer_router(h2, g, sh, sc, wq_t, keys, tiles_per_batch)
    return peer_experts(ht, u_tab.astype(BF16), v_tab.T.astype(BF16), sb, eb, th, ea, h2, gate, tiles_per_batch)


def _glu_kernel(x_ref, g_ref, sh_ref, sc_ref, w_ref, b_ref, out_ref):
    a = _rms_mod(x_ref[...], g_ref[...], sh_ref[...], sc_ref[...]).astype(BF16)
    y = _mm(a, w_ref[...]) + b_ref[...]
    d = out_ref.shape[-1]
    out_ref[...] = y[:, :d] * jax.nn.sigmoid(y[:, d:])


def conv_glu(h2, g, sh, sc, w_pw1, b_pw1, tiles_per_batch, *, tm=512):
    t, d = h2.shape
    row = pl.BlockSpec((None, 1, d), lambda i: (i // tiles_per_batch(tm), 0, 0))
    return pl.pallas_call(
        _glu_kernel,
        grid=(t // tm,),
        in_specs=[pl.BlockSpec((tm, d), lambda i: (i, 0)), pl.BlockSpec((1, d), lambda i: (0, 0)), row, row,
                  pl.BlockSpec((d, 2 * d), lambda i: (0, 0)), pl.BlockSpec((1, 2 * d), lambda i: (0, 0))],
        out_specs=pl.BlockSpec((tm, d), lambda i: (i, 0)),
        out_shape=jax.ShapeDtypeStruct((t, d), F32),
        compiler_params=_params("parallel"),
        name="conv_glu",
    )(h2, g, sh, sc, w_pw1, b_pw1)


def _conv_tail_kernel(prev_ref, cur_ref, next_ref, wdw_ref, bdw_ref, gln_ref, bln_ref, w2_ref, b2_ref,
                      h_ref, gate_ref, out_ref, win_ref, *, ts):
    s = pl.program_id(1)
    halo = CONV_HALO
    win_ref[0:halo, :] = jnp.where(s > 0, prev_ref[...], 0.0)
    win_ref[halo:halo + ts, :] = cur_ref[...]
    win_ref[halo + ts:, :] = jnp.where(s < pl.num_programs(1) - 1, next_ref[...], 0.0)
    acc = jnp.zeros((ts, cur_ref.shape[-1]), F32) + bdw_ref[...]
    base = halo - CONV_WIDTH // 2
    for k in range(CONV_WIDTH):
        acc = acc + win_ref[base + k:base + k + ts, :] * wdw_ref[k:k + 1, :]
    mu = jnp.mean(acc, axis=-1, keepdims=True)
    cen = acc - mu
    var = jnp.mean(cen * cen, axis=-1, keepdims=True)
    y = cen * lax.rsqrt(var + EPS) * gln_ref[...] + bln_ref[...]
    y = y * jax.nn.sigmoid(y)
    y = _mm(y.astype(BF16), w2_ref[...]) + b2_ref[...]
    out_ref[...] = h_ref[...] + gate_ref[...] * y


def conv_tail(a3, w_dw, b_dw, g_ln, b_ln, w_pw2, b_pw2, h3, gate, *, ts=256):
    b, s, d = a3.shape
    hb = ts // CONV_HALO
    n_halo = s // CONV_HALO
    vec = pl.BlockSpec((1, d), lambda i, j: (0, 0))
    return pl.pallas_call(
        functools.partial(_conv_tail_kernel, ts=ts),
        grid=(b, s // ts),
        in_specs=[pl.BlockSpec((None, CONV_HALO, d), lambda i, j: (i, jnp.maximum(j * hb - 1, 0), 0)),
                  pl.BlockSpec((None, ts, d), lambda i, j: (i, j, 0)),
                  pl.BlockSpec((None, CONV_HALO, d), lambda i, j: (i, jnp.minimum((j + 1) * hb, n_halo - 1), 0)),
                  pl.BlockSpec((CONV_WIDTH, d), lambda i, j: (0, 0)), vec, vec, vec,
                  pl.BlockSpec((d, d), lambda i, j: (0, 0)), vec,
                  pl.BlockSpec((None, ts, d), lambda i, j: (i, j, 0)),
                  pl.BlockSpec((None, 1, d), lambda i, j: (i, 0, 0))],
        out_specs=pl.BlockSpec((None, ts, d), lambda i, j: (i, j, 0)),
        out_shape=jax.ShapeDtypeStruct((b, s, d), F32),
        scratch_shapes=[pltpu.VMEM((ts + 2 * CONV_HALO, d), F32)],
        compiler_params=_params("parallel", "arbitrary"),
        name="conv_tail",
    )(a3, a3, a3, w_dw, b_dw, g_ln, b_ln, w_pw2, b_pw2, h3, gate)


def _pad_heads(w, width):
    k = w.shape[0]
    w = w.reshape(k, N_HEADS, width)
    return jnp.pad(w, ((0, 0), (0, 0), (0, LANES - width))).reshape(k, N_HEADS * LANES)


def _attn_weights(w_in, g_qa, w_q_up, g_kva, w_kv_up, g_q, g_k, na_g_q, na_g_k):
    a, b, c = 384, 384 + 256, 384 + 256 + MLA_ROPE
    kr_cols = jnp.pad(w_in[:, b:c], ((0, 0), (MLA_NOPE, LANES - MLA_QK)))
    w_in_r = jnp.concatenate([w_in[:, :a], w_in[:, a:b], kr_cols, w_in[:, c:]], axis=1).astype(BF16)
    wq = _pad_heads(w_q_up, MLA_QK).astype(BF16)
    kv = w_kv_up.reshape(-1, N_HEADS, MLA_NOPE + HEAD_DIM)
    wk = _pad_heads(kv[:, :, :MLA_NOPE].reshape(-1, N_HEADS * MLA_NOPE), MLA_NOPE).astype(BF16)
    wv = kv[:, :, MLA_NOPE:].reshape(-1, N_HEADS * HEAD_DIM).astype(BF16)
    pad_g = lambda g: jnp.pad(g, (0, LANES - MLA_QK)).reshape(1, LANES)
    gq = pad_g(g_q) * (MLA_QK ** -0.5)
    gk = pad_g(g_k)
    gnq = jnp.tile(na_g_q, 2).reshape(1, LANES) * (HEAD_DIM ** -0.5)
    gnk = jnp.tile(na_g_k, 2).reshape(1, LANES)
    return (w_in_r, g_qa.reshape(1, -1), wq, g_kva.reshape(1, -1), wk, wv, gq, gk, gnq, gnk)


def _rope_tables(s):
    t = jnp.arange(s)
    row = (t // GRID_W).astype(F32)
    col = (t % GRID_W).astype(F32)
    n_freq = MLA_ROPE // 4
    inv_freq = 1.0 / (ROPE_THETA ** (jnp.arange(n_freq, dtype=F32) / n_freq))
    ang = jnp.concatenate([row[:, None] * inv_freq, col[:, None] * inv_freq], axis=-1)
    cos, sin = jnp.cos(ang), jnp.sin(ang)
    half = MLA_ROPE // 2
    ones = jnp.ones((s, MLA_NOPE), F32)
    z = lambda n: jnp.zeros((s, n), F32)
    cos_t = jnp.concatenate([ones, cos, cos, z(LANES - MLA_QK)], axis=1)
    sin_lo = jnp.concatenate([z(MLA_NOPE), -sin, z(half), z(LANES - MLA_QK)], axis=1)
    sin_hi = jnp.concatenate([z(MLA_NOPE), z(half), sin, z(LANES - MLA_QK)], axis=1)
    return cos_t, sin_lo, sin_hi


def kernel(x, c, ctx, c_ctx, w_ada, b_ada, g_norm1, g_norm2, attn_w_in, mla_g_qa, mla_w_q_up, mla_g_kva, mla_w_kv_up, mla_g_q, mla_g_k, na_g_q, na_g_k, na_rpb, attn_w_out, conv_w_pw1, conv_b_pw1, conv_w_dw, conv_b_dw, conv_g_ln, conv_b_ln, conv_w_pw2, conv_b_pw2, peer_w_query, peer_sub_keys, peer_u, peer_v):
    b, s, d = x.shape
    n_ctx = ctx.shape[1]
    t = b * s
    depth = w_ada.shape[0]
    assert depth == 2
    tiles_per_batch = lambda tile: s // tile

    cond = jnp.zeros((16, d), F32).at[:b].set(c).at[b].set(c_ctx)
    mod = ada_params(cond, w_ada, b_ada).reshape(depth, 16, 6, 1, d)
    mods = [[mod[l, :, k] for k in range(6)] for l in range(depth)]

    h2 = x.reshape(t, d)
    for l in range(depth):
        sh1, sc1, g1, sh2, sc2, g2 = mods[l]
        i = l // 2
        gn1 = g_norm1[l].reshape(1, d)
        if l % 2 == 0:
            wts = _attn_weights(attn_w_in[i], mla_g_qa[i], mla_w_q_up[i], mla_g_kva[i], mla_w_kv_up[i],
                                mla_g_q[i], mla_g_k[i], na_g_q[i], na_g_k[i])
            tabs = _rope_tables(s)
            tm = 512
            qm, km, vm, qn, kn, vn = attn_inputs(
                h2, lambda j: j // (s // tm), lambda j: j % (s // tm), tm, gn1, sh1, sc1, wts, tabs, latent=True)
            tc = n_ctx
            kmc, vmc, knc, vnc = attn_inputs(
                ctx.reshape(b * n_ctx, d), lambda j: b, lambda j: 0, tc, gn1, sh1, sc1, wts,
                tuple(tb[:tc] for tb in tabs), latent=False)
            r3 = lambda arr, n: arr.reshape(b, n, arr.shape[-1])
            om = mla_attention(r3(qm, s), r3(km, s), r3(vm, s), r3(kmc, n_ctx), r3(vmc, n_ctx))
            on = na_attention(r3(qn, s), r3(kn, s), r3(vn, s), r3(knc, n_ctx), r3(vnc, n_ctx),
                              na_bias_table(na_rpb[i]))
            w_out = attn_w_out[i].astype(BF16)
            half = w_out.shape[0] // 2
            h2 = out_proj(om.reshape(t, -1), on.reshape(t, -1), w_out[:half], w_out[half:], h2, g1,
                          tiles_per_batch)
        else:
            a2 = conv_glu(h2, gn1, sh1, sc1, conv_w_pw1[i].astype(BF16), conv_b_pw1[i].reshape(1, -1),
                          tiles_per_batch)
            h2 = conv_tail(a2.reshape(b, s, d), conv_w_dw[i], conv_b_dw[i].reshape(1, d),
                           conv_g_ln[i].reshape(1, d), conv_b_ln[i].reshape(1, d),
                           conv_w_pw2[i].astype(BF16), conv_b_pw2[i].reshape(1, d),
                           h2.reshape(b, s, d), g1).reshape(t, d)
        h2 = peer_layer(h2, g_norm2[l].reshape(1, d), sh2, sc2, g2, peer_w_query[l], peer_sub_keys[l],
                        peer_u[l], peer_v[l], tiles_per_batch)
    return h2.reshape(b, s, d)
```

```python
import functools

import jax
import jax.numpy as jnp
import numpy as np
from jax import lax
from jax.experimental import pallas as pl
from jax.experimental.pallas import tpu as pltpu

F32 = jnp.float32
BF16 = jnp.bfloat16

LANES = 128
EPS = 1e-6
NEG_BIG = -1e30
VMEM_LIMIT = 56 * 1024 * 1024

GRID_W = 64
HEAD_DIM = 64
N_HEADS = 8
MLA_NOPE = 64
MLA_ROPE = 32
MLA_QK = MLA_NOPE + MLA_ROPE
NA_WIN_ROWS = 8
NA_WIN_COLS = 16
ROPE_THETA = 10000.0
CONV_WIDTH = 31
CONV_HALO = 16
PEER_HEADS = 8
PEER_N_KEYS = 128
PEER_TOPK = 16

NT_DIMS = (((1,), (1,)), ((), ()))


def _params(*sem):
    return pltpu.CompilerParams(dimension_semantics=sem, vmem_limit_bytes=VMEM_LIMIT)


def _mm(a, b):
    return jnp.dot(a, b, preferred_element_type=F32)


def _mm_nt(a, b):
    return lax.dot_general(a, b, NT_DIMS, preferred_element_type=F32)


def _rms_mod(x, g, shift, scale):
    inv = lax.rsqrt(jnp.mean(x * x, axis=-1, keepdims=True) + EPS)
    return (x * inv * g) * (1.0 + scale) + shift


def _ada_kernel(cond_ref, w_ref, b_ref, out_ref):
    cond = cond_ref[...]
    a = cond * jax.nn.sigmoid(cond)
    out_ref[...] = jnp.dot(a, w_ref[...], preferred_element_type=F32,
                           precision=lax.Precision.HIGHEST) + b_ref[...]


def ada_params(cond, w_ada, b_ada, *, tn=1536):
    depth, d, n = w_ada.shape
    rows = cond.shape[0]
    return pl.pallas_call(
        _ada_kernel,
        grid=(depth, n // tn),
        in_specs=[pl.BlockSpec((rows, d), lambda l, j: (0, 0)),
                  pl.BlockSpec((None, d, tn), lambda l, j: (l, 0, j)),
                  pl.BlockSpec((None, 1, tn), lambda l, j: (l, 0, j))],
        out_specs=pl.BlockSpec((None, rows, tn), lambda l, j: (l, 0, j)),
        out_shape=jax.ShapeDtypeStruct((depth, rows, n), F32),
        compiler_params=_params("parallel", "parallel"),
        name="ada_params",
    )(cond, w_ada, b_ada.reshape(depth, 1, n))


def _rope128(x, cos, sin_lo, sin_hi):
    return x * cos + pltpu.roll(x, LANES - MLA_ROPE // 2, 1) * sin_lo + pltpu.roll(x, MLA_ROPE // 2, 1) * sin_hi


def _head_rms128(x, n_valid):
    ss = jnp.sum(x * x, axis=-1, keepdims=True) * (1.0 / n_valid)
    return x * lax.rsqrt(ss + EPS)


def _pair_rms64(x, lane_lo):
    xx = x * x
    ss_lo = jnp.sum(jnp.where(lane_lo, xx, 0.0), axis=-1, keepdims=True)
    ss_hi = jnp.sum(jnp.where(lane_lo, 0.0, xx), axis=-1, keepdims=True)
    inv = jnp.where(lane_lo, lax.rsqrt(ss_lo * (1.0 / HEAD_DIM) + EPS), lax.rsqrt(ss_hi * (1.0 / HEAD_DIM) + EPS))
    return x * inv


def _attn_in_kernel(x_ref, g1_ref, sh_ref, sc_ref, w_in_ref, gqa_ref, wq_ref, gkva_ref, wk_ref, wv_ref,
                    gq_ref, gk_ref, gnq_ref, gnk_ref, cos_ref, slo_ref, shi_ref, *out_refs, latent):
    if latent:
        qm_ref, km_ref, vm_ref, qn_ref, kn_ref, vn_ref = out_refs
    else:
        km_ref, vm_ref, kn_ref, vn_ref = out_refs
    a = _rms_mod(x_ref[...], g1_ref[...], sh_ref[...], sc_ref[...]).astype(BF16)
    z = _mm(a, w_in_ref[...])
    c0, c1, c2 = 384, 384 + 256, 384 + 256 + LANES
    cq, ckv, kr = z[:, :c0], z[:, c0:c1], z[:, c1:c2]
    naq, nak, nav = z[:, c2:c2 + 512], z[:, c2 + 512:c2 + 1024], z[:, c2 + 1024:c2 + 1536]

    def rms(v, g):
        return v * lax.rsqrt(jnp.mean(v * v, axis=-1, keepdims=True) + EPS) * g

    ckv_n = rms(ckv, gkva_ref[...]).astype(BF16)
    kfull = _mm(ckv_n, wk_ref[...])
    vm_ref[...] = _mm(ckv_n, wv_ref[...]).astype(BF16)
    if latent:
        qfull = _mm(rms(cq, gqa_ref[...]).astype(BF16), wq_ref[...])
        cos, slo, shi = cos_ref[...], slo_ref[...], shi_ref[...]
    gq, gk = gq_ref[...], gk_ref[...]
    for h in range(N_HEADS):
        sl = slice(h * LANES, (h + 1) * LANES)
        kh = _head_rms128(kfull[:, sl] + kr, MLA_QK) * gk
        if latent:
            kh = _rope128(kh, cos, slo, shi)
            qh = _rope128(_head_rms128(qfull[:, sl], MLA_QK) * gq, cos, slo, shi)
            qm_ref[:, sl] = qh.astype(BF16)
        km_ref[:, sl] = kh.astype(BF16)
    lane_lo = lax.broadcasted_iota(jnp.int32, (1, LANES), 1) < HEAD_DIM
    gnq, gnk = gnq_ref[...], gnk_ref[...]
    for p in range(N_HEADS // 2):
        sl = slice(p * LANES, (p + 1) * LANES)
        kn_ref[:, sl] = (_pair_rms64(nak[:, sl], lane_lo) * gnk).astype(BF16)
        if latent:
            qn_ref[:, sl] = (_pair_rms64(naq[:, sl], lane_lo) * gnq).astype(BF16)
    vn_ref[...] = nav.astype(BF16)


def attn_inputs(x2, batch_of_tile, pos_of_tile, tm, g1, sh, sc, wts, rope_tabs, *, latent):
    t, d = x2.shape
    w_in, gqa, wq, gkva, wk, wv, gq, gk, gnq, gnk = wts
    full = lambda arr: pl.BlockSpec(arr.shape, lambda i: (0,) * arr.ndim)
    row = pl.BlockSpec((None, 1, d), lambda i: (batch_of_tile(i), 0, 0))
    tab = pl.BlockSpec((tm, LANES), lambda i: (pos_of_tile(i), 0))
    wide = jax.ShapeDtypeStruct((t, N_HEADS * LANES), BF16)
    half = jax.ShapeDtypeStruct((t, N_HEADS * HEAD_DIM), BF16)
    o_wide = pl.BlockSpec((tm, N_HEADS * LANES), lambda i: (i, 0))
    o_half = pl.BlockSpec((tm, N_HEADS * HEAD_DIM), lambda i: (i, 0))
    if latent:
        out_shape, out_specs = (wide, wide, half, half, half, half), (o_wide, o_wide, o_half, o_half, o_half, o_half)
    else:
        out_shape, out_specs = (wide, half, half, half), (o_wide, o_half, o_half, o_half)
    return pl.pallas_call(
        functools.partial(_attn_in_kernel, latent=latent),
        grid=(t // tm,),
        in_specs=[pl.BlockSpec((tm, d), lambda i: (i, 0)), full(g1), row, row,
                  full(w_in), full(gqa), full(wq), full(gkva), full(wk), full(wv),
                  full(gq), full(gk), full(gnq), full(gnk), tab, tab, tab],
        out_specs=out_specs,
        out_shape=out_shape,
        compiler_params=_params("parallel"),
        name="attn_inputs_lat" if latent else "attn_inputs_ctx",
    )(x2, g1, sh, sc, w_in, gqa, wq, gkva, wk, wv, gq, gk, gnq, gnk, *rope_tabs)


def _softmax_pv(s_list, v_list):
    m = functools.reduce(jnp.maximum, [jnp.max(s, axis=-1, keepdims=True) for s in s_list])
    ps = [jnp.exp(s - m) for s in s_list]
    denom = functools.reduce(jnp.add, [jnp.sum(p, axis=-1, keepdims=True) for p in ps])
    o = functools.reduce(jnp.add, [_mm(p.astype(BF16), v) for p, v in zip(ps, v_list)])
    return o / denom


def _mla_kernel(q_ref, k_ref, v_ref, kc_ref, vc_ref, o_ref):
    lane_lo = lax.broadcasted_iota(jnp.int32, (1, LANES), 1) < HEAD_DIM
    for p in range(N_HEADS // 2):
        vsl = slice(p * LANES, (p + 1) * LANES)
        outs = []
        for h in (2 * p, 2 * p + 1):
            sl = slice(h * LANES, (h + 1) * LANES)
            q = q_ref[:, sl]
            outs.append(_softmax_pv([_mm_nt(q, k_ref[:, sl]), _mm_nt(q, kc_ref[:, sl])],
                                    [v_ref[:, vsl], vc_ref[:, vsl]]))
        o_ref[:, vsl] = jnp.where(lane_lo, outs[0], outs[1]).astype(BF16)


def mla_attention(qm, km, vm, kmc, vmc, *, tq=256):
    b, s, dq = qm.shape
    c, dv = kmc.shape[1], vm.shape[2]
    return pl.pallas_call(
        _mla_kernel,
        grid=(b, s // tq),
        in_specs=[pl.BlockSpec((None, tq, dq), lambda i, j: (i, j, 0)),
                  pl.BlockSpec((None, s, dq), lambda i, j: (i, 0, 0)),
                  pl.BlockSpec((None, s, dv), lambda i, j: (i, 0, 0)),
                  pl.BlockSpec((None, c, dq), lambda i, j: (i, 0, 0)),
                  pl.BlockSpec((None, c, dv), lambda i, j: (i, 0, 0))],
        out_specs=pl.BlockSpec((None, tq, dv), lambda i, j: (i, j, 0)),
        out_shape=jax.ShapeDtypeStruct((b, s, dv), BF16),
        compiler_params=_params("parallel", "arbitrary"),
        name="mla_attention",
    )(qm, km, vm, kmc, vmc)


def _na_row0(r, n_rows):
    return jnp.clip(r - NA_WIN_ROWS // 2, 0, n_rows - NA_WIN_ROWS)


def _na_kernel(q_ref, k_ref, v_ref, kc_ref, vc_ref, bias_ref, o_ref, *, n_rows):
    band = NA_WIN_ROWS * GRID_W
    start = pl.multiple_of(_na_row0(pl.program_id(1), n_rows) * GRID_W, GRID_W)
    lane_lo = lax.broadcasted_iota(jnp.int32, (1, LANES), 1) < HEAD_DIM
    for p in range(N_HEADS // 2):
        sl = slice(p * LANES, (p + 1) * LANES)
        qp = q_ref[:, sl]
        kp, vp = k_ref[pl.ds(start, band), sl], v_ref[pl.ds(start, band), sl]
        kcp, vcp = kc_ref[:, sl], vc_ref[:, sl]
        outs = []
        for half, keep in ((0, lane_lo), (1, jnp.logical_not(lane_lo))):
            q = jnp.where(keep, qp, jnp.zeros_like(qp))
            outs.append(_softmax_pv([_mm_nt(q, kp) + bias_ref[2 * p + half], _mm_nt(q, kcp)], [vp, vcp]))
        o_ref[:, sl] = jnp.where(lane_lo, outs[0], outs[1]).astype(BF16)


def na_attention(qn, kn, vn, knc, vnc, bias):
    b, s, dh = qn.shape
    c = knc.shape[1]
    n_rows = s // GRID_W
    band = NA_WIN_ROWS * GRID_W
    whole = lambda n: pl.BlockSpec((None, n, dh), lambda i, r: (i, 0, 0))
    return pl.pallas_call(
        functools.partial(_na_kernel, n_rows=n_rows),
        grid=(b, n_rows),
        in_specs=[pl.BlockSpec((None, GRID_W, dh), lambda i, r: (i, r, 0)),
                  whole(s), whole(s), whole(c), whole(c),
                  pl.BlockSpec((None, N_HEADS, GRID_W, band), lambda i, r: (r - _na_row0(r, n_rows), 0, 0, 0))],
        out_specs=pl.BlockSpec((None, GRID_W, dh), lambda i, r: (i, r, 0)),
        out_shape=jax.ShapeDtypeStruct((b, s, dh), BF16),
        compiler_params=_params("parallel", "arbitrary"),
        name="na_attention",
    )(qn, kn, vn, knc, vnc, bias)


def na_bias_table(rpb):
    n_heads, n_rel_r, n_rel_c = rpb.shape
    q = np.arange(GRID_W)
    c0 = np.clip(q - NA_WIN_COLS // 2, 0, GRID_W - NA_WIN_COLS)
    kc = np.arange(GRID_W)
    inside = (kc[None, :] >= c0[:, None]) & (kc[None, :] < c0[:, None] + NA_WIN_COLS)
    rel_c = kc[None, :] - q[:, None] + NA_WIN_COLS - 1
    onehot = ((rel_c[None] == np.arange(n_rel_c)[:, None, None]) & inside[None]).astype(np.float32)
    toep = jnp.einsum("hrc,cqk->hrqk", rpb, jnp.asarray(onehot), precision=lax.Precision.HIGHEST)
    toep = jnp.where(inside[None, None], toep, NEG_BIG)
    per_cls = [toep[:, NA_WIN_ROWS - 1 - cls:2 * NA_WIN_ROWS - 1 - cls] for cls in range(NA_WIN_ROWS)]
    tab = jnp.stack(per_cls)
    return tab.transpose(0, 1, 3, 2, 4).reshape(NA_WIN_ROWS, n_heads, GRID_W, NA_WIN_ROWS * GRID_W)


def _out_proj_kernel(om_ref, on_ref, wa_ref, wb_ref, h_ref, gate_ref, out_ref):
    y = _mm(om_ref[...], wa_ref[...]) + _mm(on_ref[...], wb_ref[...])
    out_ref[...] = h_ref[...] + gate_ref[...] * y


def out_proj(om, on, wa, wb, h2, gate, tiles_per_batch, *, tm=512):
    t, d = h2.shape
    k = om.shape[1]
    return pl.pallas_call(
        _out_proj_kernel,
        grid=(t // tm,),
        in_specs=[pl.BlockSpec((tm, k), lambda i: (i, 0)), pl.BlockSpec((tm, k), lambda i: (i, 0)),
                  pl.BlockSpec((k, d), lambda i: (0, 0)), pl.BlockSpec((k, d), lambda i: (0, 0)),
                  pl.BlockSpec((tm, d), lambda i: (i, 0)),
                  pl.BlockSpec((None, 1, d), lambda i: (i // tiles_per_batch(tm), 0, 0))],
        out_specs=pl.BlockSpec((tm, d), lambda i: (i, 0)),
        out_shape=jax.ShapeDtypeStruct((t, d), F32),
        compiler_params=_params("parallel"),
        name="out_proj",
    )(om, on, wa, wb, h2, gate)


def _top_rows(s, k, with_rank=False):
    rows = []
    rank = jnp.full_like(s, float(k)) if with_rank else None
    for r in range(k):
        m = jnp.max(s, axis=0, keepdims=True)
        rows.append(m)
        hit = s == m
        if with_rank:
            rank = jnp.where(hit, float(r), rank)
        s = jnp.where(hit, NEG_BIG, s)
    return rows, rank


def _peer_router_kernel(x_ref, g_ref, sh_ref, sc_ref, wq_ref, keys_ref, ht_ref, rb_ref, eb_ref, ns_ref, ea_ref):
    a = _rms_mod(x_ref[...], g_ref[...], sh_ref[...], sc_ref[...])
    at = a.T.astype(BF16)
    ht_ref[...] = at
    qt = _mm(wq_ref[...], at).astype(BF16)
    k = PEER_TOPK
    for h in range(PEER_HEADS):
        sa = _mm(keys_ref[2 * h], qt[(2 * h) * LANES:(2 * h + 1) * LANES])
        sb = _mm(keys_ref[2 * h + 1], qt[(2 * h + 1) * LANES:(2 * h + 2) * LANES])
        top_a, _ = _top_rows(sa, k)
        top_b, rank_b = _top_rows(sb, k, with_rank=True)
        a_all = jnp.concatenate(top_a, axis=0)
        b_all = jnp.concatenate(top_b, axis=0)
        b_lo, b_hi = b_all[:k // 2], b_all[k // 2:]
        jr8 = lax.broadcasted_iota(jnp.int32, (k // 2, 1), 0)
        live = [top_a[0] + b_lo, top_a[0] + b_hi]
        live += [jnp.where(jr8 < k // (i + 1), top_a[i] + b_lo, NEG_BIG) for i in range(1, k)]
        work = list(live)
        tau = None
        for _ in range(k):
            tau = jnp.max(functools.reduce(jnp.maximum, work), axis=0, keepdims=True)
            work = [jnp.where(c == tau, NEG_BIG, c) for c in work]
        m = top_a[0] + top_b[0]
        z = functools.reduce(jnp.add, [jnp.sum(jnp.where(c >= tau, jnp.exp(c - m), 0.0), axis=0, keepdims=True)
                                       for c in live])
        n_sel = jnp.zeros_like(sa)
        for j in range(k):
            cut = jnp.min(jnp.where(a_all + top_b[j] >= tau, a_all, -NEG_BIG), axis=0, keepdims=True)
            n_sel = n_sel + jnp.where(sa >= cut, 1.0, 0.0)
        rb_ref[h] = rank_b
        eb_ref[h] = jnp.exp(sb - top_b[0]) / z
        ns_ref[h] = n_sel
        ea_ref[h] = jnp.exp(sa - top_a[0])


def peer_router(h2, g, sh, sc, wq_t, keys, tiles_per_batch, *, tt=256):
    t, d = h2.shape
    tab = lambda dtype: jax.ShapeDtypeStruct((PEER_HEADS, PEER_N_KEYS, t), dtype)
    tab_spec = pl.BlockSpec((PEER_HEADS, PEER_N_KEYS, tt), lambda i: (0, 0, i))
    row = pl.BlockSpec((None, 1, d), lambda i: (i // tiles_per_batch(tt), 0, 0))
    return pl.pallas_call(
        _peer_router_kernel,
        grid=(t // tt,),
        in_specs=[pl.BlockSpec((tt, d), lambda i: (i, 0)), pl.BlockSpec((1, d), lambda i: (0, 0)), row, row,
                  pl.BlockSpec(wq_t.shape, lambda i: (0, 0)), pl.BlockSpec(keys.shape, lambda i: (0, 0, 0))],
        out_specs=(pl.BlockSpec((d, tt), lambda i: (0, i)), tab_spec, tab_spec, tab_spec, tab_spec),
        out_shape=(jax.ShapeDtypeStruct((d, t), BF16), tab(F32), tab(F32), tab(F32), tab(F32)),
        compiler_params=_params("parallel"),
        name="peer_router",
    )(h2, g, sh, sc, wq_t, keys)


def _gelu_exact(x):
    return 0.5 * x * (1.0 + lax.erf(x * np.float32(np.sqrt(0.5))))


def _peer_expert_kernel(ht_ref, u_ref, vt_ref, rb_ref, eb_ref, ns_ref, ea_ref, h_ref, gate_ref, out_ref,
                        acc_ref, w_even_ref, w_odd_ref, rbp_ref, ebp_ref, *, rows_per_block, n_blocks):
    e = pl.program_id(1)
    tt = acc_ref.shape[1]

    d = acc_ref.shape[0]
    n_chunks = rows_per_block // 2
    rows_c = 2 * PEER_N_KEYS
    d_c = d // n_chunks
    jt = 16
    n_jt = PEER_N_KEYS // jt

    def apply_chunk(w_prev, c):
        rows = slice(c * d_c, (c + 1) * d_c)
        acc_ref[rows, :] += _mm(vt_ref[rows, :], w_prev[...])

    def act_chunk(c):
        rows = slice(c * rows_c, (c + 1) * rows_c)
        return _gelu_exact(_mm(u_ref[rows, :], ht_ref[...]))

    def gate_chunk(w_next, act, c):
        ils = (2 * c, 2 * c + 1)
        for lg in range(tt // LANES):
            lanes = slice(lg * LANES, (lg + 1) * LANES)
            g = [[None] * n_jt for _ in ils]
            for h in range(PEER_HEADS):
                rows16 = [(jnp.broadcast_to(ns_ref[h, il:il + 1, lanes], (jt, LANES)).astype(BF16),
                           jnp.broadcast_to(ea_ref[h, il:il + 1, lanes], (jt, LANES)).astype(BF16)) for il in ils]
                for jg in range(n_jt):
                    js = slice(jg * jt, (jg + 1) * jt)
                    rb, eb = rbp_ref[h, js, lanes], ebp_ref[h, js, lanes]
                    for n, (n_i, ea) in enumerate(rows16):
                        term = jnp.where(rb < n_i, eb, jnp.zeros_like(eb)) * ea
                        g[n][jg] = term if g[n][jg] is None else g[n][jg] + term
            for n, il in enumerate(ils):
                for jg in range(n_jt):
                    local = slice(n * PEER_N_KEYS + jg * jt, n * PEER_N_KEYS + (jg + 1) * jt)
                    rows = slice(il * PEER_N_KEYS + jg * jt, il * PEER_N_KEYS + (jg + 1) * jt)
                    w_next[rows, lanes] = g[n][jg] * act[local, lanes].astype(BF16)

    def step(w_prev, w_next):
        act = act_chunk(0) if w_next is not None else None
        for c in range(n_chunks):
            nxt = act_chunk(c + 1) if (w_next is not None and c + 1 < n_chunks) else None
            if w_prev is not None:
                apply_chunk(w_prev, c)
            if w_next is not None:
                gate_chunk(w_next, act, c)
            act = nxt

    @pl.when(e == 0)
    def _():
        acc_ref[...] = jnp.zeros_like(acc_ref)
        rbp_ref[...] = rb_ref[...].astype(BF16)
        ebp_ref[...] = eb_ref[...].astype(BF16)
        step(None, w_even_ref)

    middle = jnp.logical_and(e > 0, e < n_blocks)

    @pl.when(jnp.logical_and(middle, e % 2 == 1))
    def _():
        step(w_even_ref, w_odd_ref)

    @pl.when(jnp.logical_and(middle, e % 2 == 0))
    def _():
        step(w_odd_ref, w_even_ref)

    @pl.when(e == n_blocks)
    def _():
        step(w_even_ref if (n_blocks - 1) % 2 == 0 else w_odd_ref, None)
        out_ref[...] = h_ref[...] + gate_ref[...] * acc_ref[...].T


def peer_experts(ht, u_bf, vt_bf, sb, eb, th, ea, h2, gate, tiles_per_batch, *, tt=512, te=1024):
    t, d = h2.shape
    n_blocks = u_bf.shape[0] // te
    rows = te // PEER_N_KEYS
    last = n_blocks - 1
    full_tab = pl.BlockSpec((PEER_HEADS, PEER_N_KEYS, tt), lambda i, e: (0, 0, i))
    row_tab = pl.BlockSpec((PEER_HEADS, rows, tt), lambda i, e: (0, jnp.minimum(e, last), i))
    return pl.pallas_call(
        functools.partial(_peer_expert_kernel, rows_per_block=rows, n_blocks=n_blocks),
        grid=(t // tt, n_blocks + 1),
        in_specs=[pl.BlockSpec((d, tt), lambda i, e: (0, i)),
                  pl.BlockSpec((te, d), lambda i, e: (jnp.minimum(e, last), 0)),
                  pl.BlockSpec((d, te), lambda i, e: (0, jnp.maximum(e - 1, 0))),
                  full_tab, full_tab, row_tab, row_tab,
                  pl.BlockSpec((tt, d), lambda i, e: (i, 0)),
                  pl.BlockSpec((None, 1, d), lambda i, e: (i // tiles_per_batch(tt), 0, 0))],
        out_specs=pl.BlockSpec((tt, d), lambda i, e: (i, 0)),
        out_shape=jax.ShapeDtypeStruct((t, d), F32),
        scratch_shapes=[pltpu.VMEM((d, tt), F32), pltpu.VMEM((te, tt), BF16), pltpu.VMEM((te, tt), BF16),
                        pltpu.VMEM((PEER_HEADS, PEER_N_KEYS, tt), BF16), pltpu.VMEM((PEER_HEADS, PEER_N_KEYS, tt), BF16)],
        compiler_params=_params("parallel", "arbitrary"),
        name="peer_experts",
    )(ht, u_bf, vt_bf, sb, eb, th, ea, h2, gate)


def peer_layer(h2, g, sh, sc, gate, w_query, sub_keys, u_tab, v_tab, tiles_per_batch):
    wq_t = w_query.T.astype(BF16)
    keys = sub_keys.reshape(PEER_HEADS * 2, PEER_N_KEYS, -1).astype(BF16)
    ht, sb, eb, th, ea = peer_router(h2, g, sh, sc, wq_t, keys, tiles_per_batch)
    return peer_experts(ht, u_tab.astype(BF16), v_tab.T.astype(BF16), sb, eb, th, ea, h2, gate, tiles_per_batch)


def _glu_kernel(x_ref, g_ref, sh_ref, sc_ref, w_ref, b_ref, out_ref):
    a = _rms_mod(x_ref[...], g_ref[...], sh_ref[...], sc_ref[...]).astype(BF16)
    y = _mm(a, w_ref[...]) + b_ref[...]
    d = out_ref.shape[-1]
    out_ref[...] = y[:, :d] * jax.nn.sigmoid(y[:, d:])


def conv_glu(h2, g, sh, sc, w_pw1, b_pw1, tiles_per_batch, *, tm=512):
    t, d = h2.shape
    row = pl.BlockSpec((None, 1, d), lambda i: (i // tiles_per_batch(tm), 0, 0))
    return pl.pallas_call(
        _glu_kernel,
        grid=(t // tm,),
        in_specs=[pl.BlockSpec((tm, d), lambda i: (i, 0)), pl.BlockSpec((1, d), lambda i: (0, 0)), row, row,
                  pl.BlockSpec((d, 2 * d), lambda i: (0, 0)), pl.BlockSpec((1, 2 * d), lambda i: (0, 0))],
        out_specs=pl.BlockSpec((tm, d), lambda i: (i, 0)),
        out_shape=jax.ShapeDtypeStruct((t, d), F32),
        compiler_params=_params("parallel"),
        name="conv_glu",
    )(h2, g, sh, sc, w_pw1, b_pw1)


def _conv_tail_kernel(prev_ref, cur_ref, next_ref, wdw_ref, bdw_ref, gln_ref, bln_ref, w2_ref, b2_ref,
                      h_ref, gate_ref, out_ref, win_ref, *, ts):
    s = pl.program_id(1)
    halo = CONV_HALO
    win_ref[0:halo, :] = jnp.where(s > 0, prev_ref[...], 0.0)
    win_ref[halo:halo + ts, :] = cur_ref[...]
    win_ref[halo + ts:, :] = jnp.where(s < pl.num_programs(1) - 1, next_ref[...], 0.0)
    acc = jnp.zeros((ts, cur_ref.shape[-1]), F32) + bdw_ref[...]
    base = halo - CONV_WIDTH // 2
    for k in range(CONV_WIDTH):
        acc = acc + win_ref[base + k:base + k + ts, :] * wdw_ref[k:k + 1, :]
    mu = jnp.mean(acc, axis=-1, keepdims=True)
    cen = acc - mu
    var = jnp.mean(cen * cen, axis=-1, keepdims=True)
    y = cen * lax.rsqrt(var + EPS) * gln_ref[...] + bln_ref[...]
    y = y * jax.nn.sigmoid(y)
    y = _mm(y.astype(BF16), w2_ref[...]) + b2_ref[...]
    out_ref[...] = h_ref[...] + gate_ref[...] * y


def conv_tail(a3, w_dw, b_dw, g_ln, b_ln, w_pw2, b_pw2, h3, gate, *, ts=256):
    b, s, d = a3.shape
    hb = ts // CONV_HALO
    n_halo = s // CONV_HALO
    vec = pl.BlockSpec((1, d), lambda i, j: (0, 0))
    return pl.pallas_call(
        functools.partial(_conv_tail_kernel, ts=ts),
        grid=(b, s // ts),
        in_specs=[pl.BlockSpec((None, CONV_HALO, d), lambda i, j: (i, jnp.maximum(j * hb - 1, 0), 0)),
                  pl.BlockSpec((None, ts, d), lambda i, j: (i, j, 0)),
                  pl.BlockSpec((None, CONV_HALO, d), lambda i, j: (i, jnp.minimum((j + 1) * hb, n_halo - 1), 0)),
                  pl.BlockSpec((CONV_WIDTH, d), lambda i, j: (0, 0)), vec, vec, vec,
                  pl.BlockSpec((d, d), lambda i, j: (0, 0)), vec,
                  pl.BlockSpec((None, ts, d), lambda i, j: (i, j, 0)),
                  pl.BlockSpec((None, 1, d), lambda i, j: (i, 0, 0))],
        out_specs=pl.BlockSpec((None, ts, d), lambda i, j: (i, j, 0)),
        out_shape=jax.ShapeDtypeStruct((b, s, d), F32),
        scratch_shapes=[pltpu.VMEM((ts + 2 * CONV_HALO, d), F32)],
        compiler_params=_params("parallel", "arbitrary"),
        name="conv_tail",
    )(a3, a3, a3, w_dw, b_dw, g_ln, b_ln, w_pw2, b_pw2, h3, gate)


def _pad_heads(w, width):
    k = w.shape[0]
    w = w.reshape(k, N_HEADS, width)
    return jnp.pad(w, ((0, 0), (0, 0), (0, LANES - width))).reshape(k, N_HEADS * LANES)


def _attn_weights(w_in, g_qa, w_q_up, g_kva, w_kv_up, g_q, g_k, na_g_q, na_g_k):
    a, b, c = 384, 384 + 256, 384 + 256 + MLA_ROPE
    kr_cols = jnp.pad(w_in[:, b:c], ((0, 0), (MLA_NOPE, LANES - MLA_QK)))
    w_in_r = jnp.concatenate([w_in[:, :a], w_in[:, a:b], kr_cols, w_in[:, c:]], axis=1).astype(BF16)
    wq = _pad_heads(w_q_up, MLA_QK).astype(BF16)
    kv = w_kv_up.reshape(-1, N_HEADS, MLA_NOPE + HEAD_DIM)
    wk = _pad_heads(kv[:, :, :MLA_NOPE].reshape(-1, N_HEADS * MLA_NOPE), MLA_NOPE).astype(BF16)
    wv = kv[:, :, MLA_NOPE:].reshape(-1, N_HEADS * HEAD_DIM).astype(BF16)
    pad_g = lambda g: jnp.pad(g, (0, LANES - MLA_QK)).reshape(1, LANES)
    gq = pad_g(g_q) * (MLA_QK ** -0.5)
    gk = pad_g(g_k)
    gnq = jnp.tile(na_g_q, 2).reshape(1, LANES) * (HEAD_DIM ** -0.5)
    gnk = jnp.tile(na_g_k, 2).reshape(1, LANES)
    return (w_in_r, g_qa.reshape(1, -1), wq, g_kva.reshape(1, -1), wk, wv, gq, gk, gnq, gnk)


def _rope_tables(s):
    t = jnp.arange(s)
    row = (t // GRID_W).astype(F32)
    col = (t % GRID_W).astype(F32)
    n_freq = MLA_ROPE // 4
    inv_freq = 1.0 / (ROPE_THETA ** (jnp.arange(n_freq, dtype=F32) / n_freq))
    ang = jnp.concatenate([row[:, None] * inv_freq, col[:, None] * inv_freq], axis=-1)
    cos, sin = jnp.cos(ang), jnp.sin(ang)
    half = MLA_ROPE // 2
    ones = jnp.ones((s, MLA_NOPE), F32)
    z = lambda n: jnp.zeros((s, n), F32)
    cos_t = jnp.concatenate([ones, cos, cos, z(LANES - MLA_QK)], axis=1)
    sin_lo = jnp.concatenate([z(MLA_NOPE), -sin, z(half), z(LANES - MLA_QK)], axis=1)
    sin_hi = jnp.concatenate([z(MLA_NOPE), z(half), sin, z(LANES - MLA_QK)], axis=1)
    return cos_t, sin_lo, sin_hi


def kernel(x, c, ctx, c_ctx, w_ada, b_ada, g_norm1, g_norm2, attn_w_in, mla_g_qa, mla_w_q_up, mla_g_kva, mla_w_kv_up, mla_g_q, mla_g_k, na_g_q, na_g_k, na_rpb, attn_w_out, conv_w_pw1, conv_b_pw1, conv_w_dw, conv_b_dw, conv_g_ln, conv_b_ln, conv_w_pw2, conv_b_pw2, peer_w_query, peer_sub_keys, peer_u, peer_v):
    b, s, d = x.shape
    n_ctx = ctx.shape[1]
    t = b * s
    depth = w_ada.shape[0]
    assert depth == 2
    tiles_per_batch = lambda tile: s // tile

    cond = jnp.zeros((16, d), F32).at[:b].set(c).at[b].set(c_ctx)
    mod = ada_params(cond, w_ada, b_ada).reshape(depth, 16, 6, 1, d)
    mods = [[mod[l, :, k] for k in range(6)] for l in range(depth)]

    h2 = x.reshape(t, d)
    for l in range(depth):
        sh1, sc1, g1, sh2, sc2, g2 = mods[l]
        i = l // 2
        gn1 = g_norm1[l].reshape(1, d)
        if l % 2 == 0:
            wts = _attn_weights(attn_w_in[i], mla_g_qa[i], mla_w_q_up[i], mla_g_kva[i], mla_w_kv_up[i],
                                mla_g_q[i], mla_g_k[i], na_g_q[i], na_g_k[i])
            tabs = _rope_tables(s)
            tm = 512
            qm, km, vm, qn, kn, vn = attn_inputs(
                h2, lambda j: j // (s // tm), lambda j: j % (s // tm), tm, gn1, sh1, sc1, wts, tabs, latent=True)
            tc = n_ctx
            kmc, vmc, knc, vnc = attn_inputs(
                ctx.reshape(b * n_ctx, d), lambda j: b, lambda j: 0, tc, gn1, sh1, sc1, wts,
                tuple(tb[:tc] for tb in tabs), latent=False)
            r3 = lambda arr, n: arr.reshape(b, n, arr.shape[-1])
            om = mla_attention(r3(qm, s), r3(km, s), r3(vm, s), r3(kmc, n_ctx), r3(vmc, n_ctx))
            on = na_attention(r3(qn, s), r3(kn, s), r3(vn, s), r3(knc, n_ctx), r3(vnc, n_ctx),
                              na_bias_table(na_rpb[i]))
            w_out = attn_w_out[i].astype(BF16)
            half = w_out.shape[0] // 2
            h2 = out_proj(om.reshape(t, -1), on.reshape(t, -1), w_out[:half], w_out[half:], h2, g1,
                          tiles_per_batch)
        else:
            a2 = conv_glu(h2, gn1, sh1, sc1, conv_w_pw1[i].astype(BF16), conv_b_pw1[i].reshape(1, -1),
                          tiles_per_batch)
            h2 = conv_tail(a2.reshape(b, s, d), conv_w_dw[i], conv_b_dw[i].reshape(1, d),
                           conv_g_ln[i].reshape(1, d), conv_b_ln[i].reshape(1, d),
                           conv_w_pw2[i].astype(BF16), conv_b_pw2[i].reshape(1, d),
                           h2.reshape(b, s, d), g1).reshape(t, d)
        h2 = peer_layer(h2, g_norm2[l].reshape(1, d), sh2, sc2, g2, peer_w_query[l], peer_sub_keys[l],
                        peer_u[l], peer_v[l], tiles_per_batch)
    return h2.reshape(b, s, d)
```

```python
import functools

import jax
import jax.numpy as jnp
import numpy as np
from jax import lax
from jax.experimental import pallas as pl
from jax.experimental.pallas import tpu as pltpu

F32 = jnp.float32
BF16 = jnp.bfloat16

LANES = 128
SUBLANES = 8
EPS = 1e-6
NEG_BIG = -1e30
VMEM_LIMIT = 56 * 1024 * 1024

GRID_W = 64
HEAD_DIM = 64
N_HEADS = 8
MLA_NOPE = 64
MLA_ROPE = 32
MLA_QK = MLA_NOPE + MLA_ROPE
NA_WIN_ROWS = 8
NA_WIN_COLS = 16
ROPE_THETA = 10000.0
CONV_WIDTH = 31
CONV_HALO = 16
PEER_HEADS = 8
PEER_N_KEYS = 128
PEER_TOPK = 16

NT_DIMS = (((1,), (1,)), ((), ()))


def _params(*sem):
    return pltpu.CompilerParams(dimension_semantics=sem, vmem_limit_bytes=VMEM_LIMIT)


def _mm(a, b):
    return jnp.dot(a, b, preferred_element_type=F32)


def _mm_nt(a, b):
    return lax.dot_general(a, b, NT_DIMS, preferred_element_type=F32)


def _rms_mod(x, g, shift, scale):
    inv = lax.rsqrt(jnp.mean(x * x, axis=-1, keepdims=True) + EPS)
    return (x * inv * g) * (1.0 + scale) + shift


def _ada_kernel(cond_ref, w_ref, b_ref, out_ref):
    cond = cond_ref[...]
    a = cond * jax.nn.sigmoid(cond)
    out_ref[...] = jnp.dot(a, w_ref[...], preferred_element_type=F32,
                           precision=lax.Precision.HIGHEST) + b_ref[...]


def ada_params(cond, w_ada, b_ada, *, tn=1536):
    depth, d, n = w_ada.shape
    rows = cond.shape[0]
    return pl.pallas_call(
        _ada_kernel,
        grid=(depth, n // tn),
        in_specs=[pl.BlockSpec((rows, d), lambda l, j: (0, 0)),
                  pl.BlockSpec((None, d, tn), lambda l, j: (l, 0, j)),
                  pl.BlockSpec((None, 1, tn), lambda l, j: (l, 0, j))],
        out_specs=pl.BlockSpec((None, rows, tn), lambda l, j: (l, 0, j)),
        out_shape=jax.ShapeDtypeStruct((depth, rows, n), F32),
        compiler_params=_params("parallel", "parallel"),
        name="ada_params",
    )(cond, w_ada, b_ada.reshape(depth, 1, n))


def _rope128(x, cos, sin_lo, sin_hi):
    return x * cos + pltpu.roll(x, LANES - MLA_ROPE // 2, 1) * sin_lo + pltpu.roll(x, MLA_ROPE // 2, 1) * sin_hi


def _head_rms128(x, n_valid):
    ss = jnp.sum(x * x, axis=-1, keepdims=True) * (1.0 / n_valid)
    return x * lax.rsqrt(ss + EPS)


def _pair_rms64(x, lane_lo):
    xx = x * x
    ss_lo = jnp.sum(jnp.where(lane_lo, xx, 0.0), axis=-1, keepdims=True)
    ss_hi = jnp.sum(jnp.where(lane_lo, 0.0, xx), axis=-1, keepdims=True)
    inv = jnp.where(lane_lo, lax.rsqrt(ss_lo * (1.0 / HEAD_DIM) + EPS), lax.rsqrt(ss_hi * (1.0 / HEAD_DIM) + EPS))
    return x * inv


def _attn_in_kernel(x_ref, g1_ref, sh_ref, sc_ref, w_in_ref, gqa_ref, wq_ref, gkva_ref, wk_ref, wv_ref,
                    gq_ref, gk_ref, gnq_ref, gnk_ref, cos_ref, slo_ref, shi_ref, *out_refs, latent):
    if latent:
        qm_ref, km_ref, vm_ref, qn_ref, kn_ref, vn_ref = out_refs
    else:
        km_ref, vm_ref, kn_ref, vn_ref = out_refs
    a = _rms_mod(x_ref[...], g1_ref[...], sh_ref[...], sc_ref[...]).astype(BF16)
    z = _mm(a, w_in_ref[...])
    c0, c1, c2 = 384, 384 + 256, 384 + 256 + LANES
    cq, ckv, kr = z[:, :c0], z[:, c0:c1], z[:, c1:c2]
    naq, nak, nav = z[:, c2:c2 + 512], z[:, c2 + 512:c2 + 1024], z[:, c2 + 1024:c2 + 1536]

    def rms(v, g):
        return v * lax.rsqrt(jnp.mean(v * v, axis=-1, keepdims=True) + EPS) * g

    ckv_n = rms(ckv, gkva_ref[...]).astype(BF16)
    kfull = _mm(ckv_n, wk_ref[...])
    vm_ref[...] = _mm(ckv_n, wv_ref[...]).astype(BF16)
    if latent:
        qfull = _mm(rms(cq, gqa_ref[...]).astype(BF16), wq_ref[...])
        cos, slo, shi = cos_ref[...], slo_ref[...], shi_ref[...]
    gq, gk = gq_ref[...], gk_ref[...]
    for h in range(N_HEADS):
        sl = slice(h * LANES, (h + 1) * LANES)
        kh = _head_rms128(kfull[:, sl] + kr, MLA_QK) * gk
        if latent:
            kh = _rope128(kh, cos, slo, shi)
            qh = _rope128(_head_rms128(qfull[:, sl], MLA_QK) * gq, cos, slo, shi)
            qm_ref[:, sl] = qh.astype(BF16)
        km_ref[:, sl] = kh.astype(BF16)
    lane_lo = lax.broadcasted_iota(jnp.int32, (1, LANES), 1) < HEAD_DIM
    gnq, gnk = gnq_ref[...], gnk_ref[...]
    for p in range(N_HEADS // 2):
        sl = slice(p * LANES, (p + 1) * LANES)
        kn_ref[:, sl] = (_pair_rms64(nak[:, sl], lane_lo) * gnk).astype(BF16)
        if latent:
            qn_ref[:, sl] = (_pair_rms64(naq[:, sl], lane_lo) * gnq).astype(BF16)
    vn_ref[...] = nav.astype(BF16)


def attn_inputs(x2, batch_of_tile, pos_of_tile, tm, g1, sh, sc, wts, rope_tabs, *, latent):
    t, d = x2.shape
    w_in, gqa, wq, gkva, wk, wv, gq, gk, gnq, gnk = wts
    full = lambda arr: pl.BlockSpec(arr.shape, lambda i: (0,) * arr.ndim)
    row = pl.BlockSpec((None, 1, d), lambda i: (batch_of_tile(i), 0, 0))
    tab = pl.BlockSpec((tm, LANES), lambda i: (pos_of_tile(i), 0))
    wide = jax.ShapeDtypeStruct((t, N_HEADS * LANES), BF16)
    half = jax.ShapeDtypeStruct((t, N_HEADS * HEAD_DIM), BF16)
    o_wide = pl.BlockSpec((tm, N_HEADS * LANES), lambda i: (i, 0))
    o_half = pl.BlockSpec((tm, N_HEADS * HEAD_DIM), lambda i: (i, 0))
    if latent:
        out_shape, out_specs = (wide, wide, half, half, half, half), (o_wide, o_wide, o_half, o_half, o_half, o_half)
    else:
        out_shape, out_specs = (wide, half, half, half), (o_wide, o_half, o_half, o_half)
    return pl.pallas_call(
        functools.partial(_attn_in_kernel, latent=latent),
        grid=(t // tm,),
        in_specs=[pl.BlockSpec((tm, d), lambda i: (i, 0)), full(g1), row, row,
                  full(w_in), full(gqa), full(wq), full(gkva), full(wk), full(wv),
                  full(gq), full(gk), full(gnq), full(gnk), tab, tab, tab],
        out_specs=out_specs,
        out_shape=out_shape,
        compiler_params=_params("parallel"),
        name="attn_inputs_lat" if latent else "attn_inputs_ctx",
    )(x2, g1, sh, sc, w_in, gqa, wq, gkva, wk, wv, gq, gk, gnq, gnk, *rope_tabs)


def _softmax_pv(s_list, v_list):
    m = functools.reduce(jnp.maximum, [jnp.max(s, axis=-1, keepdims=True) for s in s_list])
    ps = [jnp.exp(s - m) for s in s_list]
    denom = functools.reduce(jnp.add, [jnp.sum(p, axis=-1, keepdims=True) for p in ps])
    o = functools.reduce(jnp.add, [_mm(p.astype(BF16), v) for p, v in zip(ps, v_list)])
    return o / denom


def _mla_kernel(q_ref, k_ref, v_ref, kc_ref, vc_ref, o_ref):
    lane_lo = lax.broadcasted_iota(jnp.int32, (1, LANES), 1) < HEAD_DIM
    for p in range(N_HEADS // 2):
        vsl = slice(p * LANES, (p + 1) * LANES)
        outs = []
        for h in (2 * p, 2 * p + 1):
            sl = slice(h * LANES, (h + 1) * LANES)
            q = q_ref[:, sl]
            outs.append(_softmax_pv([_mm_nt(q, k_ref[:, sl]), _mm_nt(q, kc_ref[:, sl])],
                                    [v_ref[:, vsl], vc_ref[:, vsl]]))
        o_ref[:, vsl] = jnp.where(lane_lo, outs[0], outs[1]).astype(BF16)


def mla_attention(qm, km, vm, kmc, vmc, *, tq=256):
    b, s, dq = qm.shape
    c, dv = kmc.shape[1], vm.shape[2]
    return pl.pallas_call(
        _mla_kernel,
        grid=(b, s // tq),
        in_specs=[pl.BlockSpec((None, tq, dq), lambda i, j: (i, j, 0)),
                  pl.BlockSpec((None, s, dq), lambda i, j: (i, 0, 0)),
                  pl.BlockSpec((None, s, dv), lambda i, j: (i, 0, 0)),
                  pl.BlockSpec((None, c, dq), lambda i, j: (i, 0, 0)),
                  pl.BlockSpec((None, c, dv), lambda i, j: (i, 0, 0))],
        out_specs=pl.BlockSpec((None, tq, dv), lambda i, j: (i, j, 0)),
        out_shape=jax.ShapeDtypeStruct((b, s, dv), BF16),
        compiler_params=_params("parallel", "arbitrary"),
        name="mla_attention",
    )(qm, km, vm, kmc, vmc)


def _na_row0(r, n_rows):
    return jnp.clip(r - NA_WIN_ROWS // 2, 0, n_rows - NA_WIN_ROWS)


def _na_kernel(q_ref, k_ref, v_ref, kc_ref, vc_ref, bias_ref, o_ref, *, n_rows, rows_per_step):
    band = NA_WIN_ROWS * GRID_W
    lane_lo = lax.broadcasted_iota(jnp.int32, (1, LANES), 1) < HEAD_DIM
    for rr in range(rows_per_step):
        r = pl.program_id(1) * rows_per_step + rr
        row0 = _na_row0(r, n_rows)
        start = pl.multiple_of(row0 * GRID_W, GRID_W)
        cls = r - row0
        qrows = slice(rr * GRID_W, (rr + 1) * GRID_W)
        for p in range(N_HEADS // 2):
            sl = slice(p * LANES, (p + 1) * LANES)
            qp = q_ref[qrows, sl]
            kp, vp = k_ref[pl.ds(start, band), sl], v_ref[pl.ds(start, band), sl]
            kcp, vcp = kc_ref[:, sl], vc_ref[:, sl]
            outs = []
            for half, keep in ((0, lane_lo), (1, jnp.logical_not(lane_lo))):
                q = jnp.where(keep, qp, jnp.zeros_like(qp))
                outs.append(_softmax_pv([_mm_nt(q, kp) + bias_ref[cls, 2 * p + half], _mm_nt(q, kcp)], [vp, vcp]))
            o_ref[qrows, sl] = jnp.where(lane_lo, outs[0], outs[1]).astype(BF16)


def na_attention(qn, kn, vn, knc, vnc, bias, *, rows_per_step=4):
    b, s, dh = qn.shape
    c = knc.shape[1]
    n_rows = s // GRID_W
    tq = rows_per_step * GRID_W
    whole = lambda n: pl.BlockSpec((None, n, dh), lambda i, r: (i, 0, 0))
    return pl.pallas_call(
        functools.partial(_na_kernel, n_rows=n_rows, rows_per_step=rows_per_step),
        grid=(b, n_rows // rows_per_step),
        in_specs=[pl.BlockSpec((None, tq, dh), lambda i, r: (i, r, 0)),
                  whole(s), whole(s), whole(c), whole(c),
                  pl.BlockSpec(bias.shape, lambda i, r: (0, 0, 0, 0))],
        out_specs=pl.BlockSpec((None, tq, dh), lambda i, r: (i, r, 0)),
        out_shape=jax.ShapeDtypeStruct((b, s, dh), BF16),
        compiler_params=_params("parallel", "arbitrary"),
        name="na_attention",
    )(qn, kn, vn, knc, vnc, bias)


def na_bias_table(rpb):
    n_heads, n_rel_r, n_rel_c = rpb.shape
    q = np.arange(GRID_W)
    c0 = np.clip(q - NA_WIN_COLS // 2, 0, GRID_W - NA_WIN_COLS)
    kc = np.arange(GRID_W)
    inside = (kc[None, :] >= c0[:, None]) & (kc[None, :] < c0[:, None] + NA_WIN_COLS)
    rel_c = kc[None, :] - q[:, None] + NA_WIN_COLS - 1
    onehot = ((rel_c[None] == np.arange(n_rel_c)[:, None, None]) & inside[None]).astype(np.float32)
    toep = jnp.einsum("hrc,cqk->hrqk", rpb, jnp.asarray(onehot), precision=lax.Precision.HIGHEST)
    toep = jnp.where(inside[None, None], toep, NEG_BIG)
    per_cls = [toep[:, NA_WIN_ROWS - 1 - cls:2 * NA_WIN_ROWS - 1 - cls] for cls in range(NA_WIN_ROWS)]
    tab = jnp.stack(per_cls)
    return tab.transpose(0, 1, 3, 2, 4).reshape(NA_WIN_ROWS, n_heads, GRID_W, NA_WIN_ROWS * GRID_W)


def _out_proj_kernel(om_ref, on_ref, wa_ref, wb_ref, h_ref, gate_ref, out_ref):
    y = _mm(om_ref[...], wa_ref[...]) + _mm(on_ref[...], wb_ref[...])
    out_ref[...] = h_ref[...] + gate_ref[...] * y


def out_proj(om, on, wa, wb, h2, gate, tiles_per_batch, *, tm=512):
    t, d = h2.shape
    k = om.shape[1]
    return pl.pallas_call(
        _out_proj_kernel,
        grid=(t // tm,),
        in_specs=[pl.BlockSpec((tm, k), lambda i: (i, 0)), pl.BlockSpec((tm, k), lambda i: (i, 0)),
                  pl.BlockSpec((k, d), lambda i: (0, 0)), pl.BlockSpec((k, d), lambda i: (0, 0)),
                  pl.BlockSpec((tm, d), lambda i: (i, 0)),
                  pl.BlockSpec((None, 1, d), lambda i: (i // tiles_per_batch(tm), 0, 0))],
        out_specs=pl.BlockSpec((tm, d), lambda i: (i, 0)),
        out_shape=jax.ShapeDtypeStruct((t, d), F32),
        compiler_params=_params("parallel"),
        name="out_proj",
    )(om, on, wa, wb, h2, gate)


def _top_rows(s, k, with_rank=False):
    rows = []
    rank = jnp.full_like(s, float(k)) if with_rank else None
    for r in range(k):
        m = jnp.max(s, axis=0, keepdims=True)
        rows.append(m)
        hit = s == m
        if with_rank:
            rank = jnp.where(hit, float(r), rank)
        s = jnp.where(hit, NEG_BIG, s)
    return rows, rank


def _peer_router_kernel(x_ref, g_ref, sh_ref, sc_ref, wq_ref, keys_ref, ht_ref, rb_ref, eb_ref, ns_ref, ea_ref):
    a = _rms_mod(x_ref[...], g_ref[...], sh_ref[...], sc_ref[...])
    at = a.T.astype(BF16)
    ht_ref[...] = at
    qt = _mm(wq_ref[...], at).astype(BF16)
    k = PEER_TOPK
    for h in range(PEER_HEADS):
        sa = _mm(keys_ref[2 * h], qt[(2 * h) * LANES:(2 * h + 1) * LANES])
        sb = _mm(keys_ref[2 * h + 1], qt[(2 * h + 1) * LANES:(2 * h + 2) * LANES])
        top_a, _ = _top_rows(sa, k)
        top_b, rank_b = _top_rows(sb, k, with_rank=True)
        a_all = jnp.concatenate(top_a, axis=0)
        b_all = jnp.concatenate(top_b, axis=0)
        b_lo, b_hi = b_all[:k // 2], b_all[k // 2:]
        jr8 = lax.broadcasted_iota(jnp.int32, (k // 2, 1), 0)
        live = [top_a[0] + b_lo, top_a[0] + b_hi]
        live += [jnp.where(jr8 < k // (i + 1), top_a[i] + b_lo, NEG_BIG) for i in range(1, k)]
        work = list(live)
        tau = None
        for _ in range(k):
            tau = jnp.max(functools.reduce(jnp.maximum, work), axis=0, keepdims=True)
            work = [jnp.where(c == tau, NEG_BIG, c) for c in work]
        m = top_a[0] + top_b[0]
        z = functools.reduce(jnp.add, [jnp.sum(jnp.where(c >= tau, jnp.exp(c - m), 0.0), axis=0, keepdims=True)
                                       for c in live])
        n_sel = jnp.zeros_like(sa)
        for j in range(k):
            cut = jnp.min(jnp.where(a_all + top_b[j] >= tau, a_all, -NEG_BIG), axis=0, keepdims=True)
            n_sel = n_sel + jnp.where(sa >= cut, 1.0, 0.0)
        rb_ref[h] = rank_b
        eb_ref[h] = jnp.exp(sb - top_b[0]) / z
        ns_ref[h] = n_sel
        ea_ref[h] = jnp.exp(sa - top_a[0])


def peer_router(h2, g, sh, sc, wq_t, keys, tiles_per_batch, *, tt=256):
    t, d = h2.shape
    tab = lambda dtype: jax.ShapeDtypeStruct((PEER_HEADS, PEER_N_KEYS, t), dtype)
    tab_spec = pl.BlockSpec((PEER_HEADS, PEER_N_KEYS, tt), lambda i: (0, 0, i))
    row = pl.BlockSpec((None, 1, d), lambda i: (i // tiles_per_batch(tt), 0, 0))
    return pl.pallas_call(
        _peer_router_kernel,
        grid=(t // tt,),
        in_specs=[pl.BlockSpec((tt, d), lambda i: (i, 0)), pl.BlockSpec((1, d), lambda i: (0, 0)), row, row,
                  pl.BlockSpec(wq_t.shape, lambda i: (0, 0)), pl.BlockSpec(keys.shape, lambda i: (0, 0, 0))],
        out_specs=(pl.BlockSpec((d, tt), lambda i: (0, i)), tab_spec, tab_spec, tab_spec, tab_spec),
        out_shape=(jax.ShapeDtypeStruct((d, t), BF16), tab(F32), tab(F32), tab(F32), tab(F32)),
        compiler_params=_params("parallel"),
        name="peer_router",
    )(h2, g, sh, sc, wq_t, keys)


def _gelu_exact(x):
    return 0.5 * x * (1.0 + lax.erf(x * np.float32(np.sqrt(0.5))))


def _peer_expert_kernel(ht_ref, u_ref, vt_ref, rb_ref, eb_ref, ns_ref, ea_ref, h_ref, gate_ref, out_ref,
                        acc_ref, w_ref, *, rows_per_block):
    e = pl.program_id(1)
    tt = acc_ref.shape[1]
    n_chunks = rows_per_block // 2
    rows_c = 2 * PEER_N_KEYS
    jt = 2 * SUBLANES

    def score_chunk(c):
        r0 = c * rows_c
        act = _gelu_exact(_mm(u_ref[r0:r0 + rows_c, :], ht_ref[...]))
        for lg in range(tt // LANES):
            lanes = slice(lg * LANES, (lg + 1) * LANES)
            for n in range(2):
                il = 2 * c + n
                rows = [(jnp.maximum(jnp.broadcast_to(ns_ref[h, il:il + 1, lanes], (SUBLANES, LANES)), 0.0),
                         jnp.maximum(jnp.broadcast_to(ea_ref[h, il:il + 1, lanes], (SUBLANES, LANES)), 0.0))
                        for h in range(PEER_HEADS)]
                for j0 in range(0, PEER_N_KEYS, jt):
                    halves = []
                    for k in range(jt // SUBLANES):
                        js = slice(j0 + k * SUBLANES, j0 + (k + 1) * SUBLANES)
                        g = None
                        for h, (n_i, ea) in enumerate(rows):
                            term = jnp.where(rb_ref[h, js, lanes] < n_i, eb_ref[h, js, lanes], 0.0) * ea
                            g = term if g is None else g + term
                        off = n * PEER_N_KEYS + j0 + k * SUBLANES
                        halves.append(g * act[off:off + SUBLANES, lanes])
                    off = n * PEER_N_KEYS + j0
                    w_ref[r0 + off:r0 + off + jt, lanes] = jnp.concatenate(halves, axis=0).astype(BF16)

    @pl.when(e == 0)
    def _():
        acc_ref[...] = jnp.zeros_like(acc_ref)

    for c in range(n_chunks):
        score_chunk(c)
    acc_ref[...] += _mm(vt_ref[...], w_ref[...])

    @pl.when(e == pl.num_programs(1) - 1)
    def _():
        out_ref[...] = h_ref[...] + gate_ref[...] * acc_ref[...].T


def peer_experts(ht, u_bf, vt_bf, rb, eb, ns, ea, h2, gate, tiles_per_batch, *, tt=512, te=2048):
    t, d = h2.shape
    n_blocks = u_bf.shape[0] // te
    rows = te // PEER_N_KEYS
    full_tab = pl.BlockSpec((PEER_HEADS, PEER_N_KEYS, tt), lambda i, e: (0, 0, i))
    row_tab = pl.BlockSpec((PEER_HEADS, rows, tt), lambda i, e: (0, e, i))
    return pl.pallas_call(
        functools.partial(_peer_expert_kernel, rows_per_block=rows),
        grid=(t // tt, n_blocks),
        in_specs=[pl.BlockSpec((d, tt), lambda i, e: (0, i)),
                  pl.BlockSpec((te, d), lambda i, e: (e, 0)),
                  pl.BlockSpec((d, te), lambda i, e: (0, e)),
                  full_tab, full_tab, row_tab, row_tab,
                  pl.BlockSpec((tt, d), lambda i, e: (i, 0)),
                  pl.BlockSpec((None, 1, d), lambda i, e: (i // tiles_per_batch(tt), 0, 0))],
        out_specs=pl.BlockSpec((tt, d), lambda i, e: (i, 0)),
        out_shape=jax.ShapeDtypeStruct((t, d), F32),
        scratch_shapes=[pltpu.VMEM((d, tt), F32), pltpu.VMEM((te, tt), BF16)],
        compiler_params=_params("parallel", "arbitrary"),
        name="peer_experts",
    )(ht, u_bf, vt_bf, rb, eb, ns, ea, h2, gate)


def peer_layer(h2, g, sh, sc, gate, w_query, sub_keys, u_tab, v_tab, tiles_per_batch):
    wq_t = w_query.T.astype(BF16)
    keys = sub_keys.reshape(PEER_HEADS * 2, PEER_N_KEYS, -1).astype(BF16)
    ht, sb, eb, th, ea = peer_router(h2, g, sh, sc, wq_t, keys, tiles_per_batch)
    return peer_experts(ht, u_tab.astype(BF16), v_tab.T.astype(BF16), sb, eb, th, ea, h2, gate, tiles_per_batch)


def _glu_kernel(x_ref, g_ref, sh_ref, sc_ref, w_ref, b_ref, out_ref):
    a = _rms_mod(x_ref[...], g_ref[...], sh_ref[...], sc_ref[...]).astype(BF16)
    y = _mm(a, w_ref[...]) + b_ref[...]
    d = out_ref.shape[-1]
    out_ref[...] = y[:, :d] * jax.nn.sigmoid(y[:, d:])


def conv_glu(h2, g, sh, sc, w_pw1, b_pw1, tiles_per_batch, *, tm=512):
    t, d = h2.shape
    row = pl.BlockSpec((None, 1, d), lambda i: (i // tiles_per_batch(tm), 0, 0))
    return pl.pallas_call(
        _glu_kernel,
        grid=(t // tm,),
        in_specs=[pl.BlockSpec((tm, d), lambda i: (i, 0)), pl.BlockSpec((1, d), lambda i: (0, 0)), row, row,
                  pl.BlockSpec((d, 2 * d), lambda i: (0, 0)), pl.BlockSpec((1, 2 * d), lambda i: (0, 0))],
        out_specs=pl.BlockSpec((tm, d), lambda i: (i, 0)),
        out_shape=jax.ShapeDtypeStruct((t, d), F32),
        compiler_params=_params("parallel"),
        name="conv_glu",
    )(h2, g, sh, sc, w_pw1, b_pw1)


def _conv_tail_kernel(prev_ref, cur_ref, next_ref, wdw_ref, bdw_ref, gln_ref, bln_ref, w2_ref, b2_ref,
                      h_ref, gate_ref, out_ref, win_ref, shift_ref, *, ts):
    s = pl.program_id(1)
    halo = CONV_HALO
    win_ref[0:halo, :] = jnp.where(s > 0, prev_ref[...], 0.0)
    win_ref[halo:halo + ts, :] = cur_ref[...]
    win_ref[halo + ts:, :] = jnp.where(s < pl.num_programs(1) - 1, next_ref[...], 0.0)
    acc = jnp.zeros((ts, cur_ref.shape[-1]), F32) + bdw_ref[...]
    base = halo - CONV_WIDTH // 2
    span = shift_ref.shape[0]
    for r in range(SUBLANES):
        src = win_ref
        if r:
            shift_ref[...] = win_ref[r:r + span, :]
            src = shift_ref
        for a in range(0, span - ts + 1, SUBLANES):
            k = a + r - base
            if 0 <= k < CONV_WIDTH:
                acc = acc + src[a:a + ts, :] * wdw_ref[k:k + 1, :]
    mu = jnp.mean(acc, axis=-1, keepdims=True)
    cen = acc - mu
    var = jnp.mean(cen * cen, axis=-1, keepdims=True)
    y = cen * lax.rsqrt(var + EPS) * gln_ref[...] + bln_ref[...]
    y = y * jax.nn.sigmoid(y)
    y = _mm(y.astype(BF16), w2_ref[...]) + b2_ref[...]
    out_ref[...] = h_ref[...] + gate_ref[...] * y


def conv_tail(a3, w_dw, b_dw, g_ln, b_ln, w_pw2, b_pw2, h3, gate, *, ts=256):
    b, s, d = a3.shape
    hb = ts // CONV_HALO
    n_halo = s // CONV_HALO
    vec = pl.BlockSpec((1, d), lambda i, j: (0, 0))
    return pl.pallas_call(
        functools.partial(_conv_tail_kernel, ts=ts),
        grid=(b, s // ts),
        in_specs=[pl.BlockSpec((None, CONV_HALO, d), lambda i, j: (i, jnp.maximum(j * hb - 1, 0), 0)),
                  pl.BlockSpec((None, ts, d), lambda i, j: (i, j, 0)),
                  pl.BlockSpec((None, CONV_HALO, d), lambda i, j: (i, jnp.minimum((j + 1) * hb, n_halo - 1), 0)),
                  pl.BlockSpec((CONV_WIDTH, d), lambda i, j: (0, 0)), vec, vec, vec,
                  pl.BlockSpec((d, d), lambda i, j: (0, 0)), vec,
                  pl.BlockSpec((None, ts, d), lambda i, j: (i, j, 0)),
                  pl.BlockSpec((None, 1, d), lambda i, j: (i, 0, 0))],
        out_specs=pl.BlockSpec((None, ts, d), lambda i, j: (i, j, 0)),
        out_shape=jax.ShapeDtypeStruct((b, s, d), F32),
        scratch_shapes=[pltpu.VMEM((ts + 2 * CONV_HALO, d), F32),
                        pltpu.VMEM((ts + 2 * CONV_HALO - SUBLANES, d), F32)],
        compiler_params=_params("parallel", "arbitrary"),
        name="conv_tail",
    )(a3, a3, a3, w_dw, b_dw, g_ln, b_ln, w_pw2, b_pw2, h3, gate)


def _pad_heads(w, width):
    k = w.shape[0]
    w = w.reshape(k, N_HEADS, width)
    return jnp.pad(w, ((0, 0), (0, 0), (0, LANES - width))).reshape(k, N_HEADS * LANES)


def _attn_weights(w_in, g_qa, w_q_up, g_kva, w_kv_up, g_q, g_k, na_g_q, na_g_k):
    a, b, c = 384, 384 + 256, 384 + 256 + MLA_ROPE
    kr_cols = jnp.pad(w_in[:, b:c], ((0, 0), (MLA_NOPE, LANES - MLA_QK)))
    w_in_r = jnp.concatenate([w_in[:, :a], w_in[:, a:b], kr_cols, w_in[:, c:]], axis=1).astype(BF16)
    wq = _pad_heads(w_q_up, MLA_QK).astype(BF16)
    kv = w_kv_up.reshape(-1, N_HEADS, MLA_NOPE + HEAD_DIM)
    wk = _pad_heads(kv[:, :, :MLA_NOPE].reshape(-1, N_HEADS * MLA_NOPE), MLA_NOPE).astype(BF16)
    wv = kv[:, :, MLA_NOPE:].reshape(-1, N_HEADS * HEAD_DIM).astype(BF16)
    pad_g = lambda g: jnp.pad(g, (0, LANES - MLA_QK)).reshape(1, LANES)
    gq = pad_g(g_q) * (MLA_QK ** -0.5)
    gk = pad_g(g_k)
    gnq = jnp.tile(na_g_q, 2).reshape(1, LANES) * (HEAD_DIM ** -0.5)
    gnk = jnp.tile(na_g_k, 2).reshape(1, LANES)
    return (w_in_r, g_qa.reshape(1, -1), wq, g_kva.reshape(1, -1), wk, wv, gq, gk, gnq, gnk)


def _rope_tables(s):
    t = jnp.arange(s)
    row = (t // GRID_W).astype(F32)
    col = (t % GRID_W).astype(F32)
    n_freq = MLA_ROPE // 4
    inv_freq = 1.0 / (ROPE_THETA ** (jnp.arange(n_freq, dtype=F32) / n_freq))
    ang = jnp.concatenate([row[:, None] * inv_freq, col[:, None] * inv_freq], axis=-1)
    cos, sin = jnp.cos(ang), jnp.sin(ang)
    half = MLA_ROPE // 2
    ones = jnp.ones((s, MLA_NOPE), F32)
    z = lambda n: jnp.zeros((s, n), F32)
    cos_t = jnp.concatenate([ones, cos, cos, z(LANES - MLA_QK)], axis=1)
    sin_lo = jnp.concatenate([z(MLA_NOPE), -sin, z(half), z(LANES - MLA_QK)], axis=1)
    sin_hi = jnp.concatenate([z(MLA_NOPE), z(half), sin, z(LANES - MLA_QK)], axis=1)
    return cos_t, sin_lo, sin_hi


def kernel(x, c, ctx, c_ctx, w_ada, b_ada, g_norm1, g_norm2, attn_w_in, mla_g_qa, mla_w_q_up, mla_g_kva, mla_w_kv_up, mla_g_q, mla_g_k, na_g_q, na_g_k, na_rpb, attn_w_out, conv_w_pw1, conv_b_pw1, conv_w_dw, conv_b_dw, conv_g_ln, conv_b_ln, conv_w_pw2, conv_b_pw2, peer_w_query, peer_sub_keys, peer_u, peer_v):
    b, s, d = x.shape
    n_ctx = ctx.shape[1]
    t = b * s
    depth = w_ada.shape[0]
    assert depth == 2
    tiles_per_batch = lambda tile: s // tile

    cond = jnp.zeros((16, d), F32).at[:b].set(c).at[b].set(c_ctx)
    mod = ada_params(cond, w_ada, b_ada).reshape(depth, 16, 6, 1, d)
    mods = [[mod[l, :, k] for k in range(6)] for l in range(depth)]

    h2 = x.reshape(t, d)
    for l in range(depth):
        sh1, sc1, g1, sh2, sc2, g2 = mods[l]
        i = l // 2
        gn1 = g_norm1[l].reshape(1, d)
        if l % 2 == 0:
            wts = _attn_weights(attn_w_in[i], mla_g_qa[i], mla_w_q_up[i], mla_g_kva[i], mla_w_kv_up[i],
                                mla_g_q[i], mla_g_k[i], na_g_q[i], na_g_k[i])
            tabs = _rope_tables(s)
            tm = 512
            qm, km, vm, qn, kn, vn = attn_inputs(
                h2, lambda j: j // (s // tm), lambda j: j % (s // tm), tm, gn1, sh1, sc1, wts, tabs, latent=True)
            tc = n_ctx
            kmc, vmc, knc, vnc = attn_inputs(
                ctx.reshape(b * n_ctx, d), lambda j: b, lambda j: 0, tc, gn1, sh1, sc1, wts,
                tuple(tb[:tc] for tb in tabs), latent=False)
            r3 = lambda arr, n: arr.reshape(b, n, arr.shape[-1])
            om = mla_attention(r3(qm, s), r3(km, s), r3(vm, s), r3(kmc, n_ctx), r3(vmc, n_ctx))
            on = na_attention(r3(qn, s), r3(kn, s), r3(vn, s), r3(knc, n_ctx), r3(vnc, n_ctx),
                              na_bias_table(na_rpb[i]))
            w_out = attn_w_out[i].astype(BF16)
            half = w_out.shape[0] // 2
            h2 = out_proj(om.reshape(t, -1), on.reshape(t, -1), w_out[:half], w_out[half:], h2, g1,
                          tiles_per_batch)
        else:
            a2 = conv_glu(h2, gn1, sh1, sc1, conv_w_pw1[i].astype(BF16), conv_b_pw1[i].reshape(1, -1),
                          tiles_per_batch)
            h2 = conv_tail(a2.reshape(b, s, d), conv_w_dw[i], conv_b_dw[i].reshape(1, d),
                           conv_g_ln[i].reshape(1, d), conv_b_ln[i].reshape(1, d),
                           conv_w_pw2[i].astype(BF16), conv_b_pw2[i].reshape(1, d),
                           h2.reshape(b, s, d), g1).reshape(t, d)
        h2 = peer_layer(h2, g_norm2[l].reshape(1, d), sh2, sc2, g2, peer_w_query[l], peer_sub_keys[l],
                        peer_u[l], peer_v[l], tiles_per_batch)
    return h2.reshape(b, s, d)
```

```python
import functools

import jax
import jax.numpy as jnp
import numpy as np
from jax import lax
from jax.experimental import pallas as pl
from jax.experimental.pallas import tpu as pltpu

F32 = jnp.float32
BF16 = jnp.bfloat16

LANES = 128
SUBLANES = 8
EPS = 1e-6
NEG_BIG = -1e30
VMEM_LIMIT = 56 * 1024 * 1024

GRID_W = 64
HEAD_DIM = 64
N_HEADS = 8
MLA_NOPE = 64
MLA_ROPE = 32
MLA_QK = MLA_NOPE + MLA_ROPE
NA_WIN_ROWS = 8
NA_WIN_COLS = 16
ROPE_THETA = 10000.0
CONV_WIDTH = 31
CONV_HALO = 16
PEER_HEADS = 8
PEER_N_KEYS = 128
PEER_TOPK = 16

NT_DIMS = (((1,), (1,)), ((), ()))


def _params(*sem):
    return pltpu.CompilerParams(dimension_semantics=sem, vmem_limit_bytes=VMEM_LIMIT)


def _mm(a, b):
    return jnp.dot(a, b, preferred_element_type=F32)


def _mm_nt(a, b):
    return lax.dot_general(a, b, NT_DIMS, preferred_element_type=F32)


def _rms_mod(x, g, shift, scale):
    inv = lax.rsqrt(jnp.mean(x * x, axis=-1, keepdims=True) + EPS)
    return (x * inv * g) * (1.0 + scale) + shift


def _ada_kernel(cond_ref, w_ref, b_ref, out_ref):
    cond = cond_ref[...]
    a = cond * jax.nn.sigmoid(cond)
    out_ref[...] = jnp.dot(a, w_ref[...], preferred_element_type=F32,
                           precision=lax.Precision.HIGHEST) + b_ref[...]


def ada_params(cond, w_ada, b_ada, *, tn=1536):
    depth, d, n = w_ada.shape
    rows = cond.shape[0]
    return pl.pallas_call(
        _ada_kernel,
        grid=(depth, n // tn),
        in_specs=[pl.BlockSpec((rows, d), lambda l, j: (0, 0)),
                  pl.BlockSpec((None, d, tn), lambda l, j: (l, 0, j)),
                  pl.BlockSpec((None, 1, tn), lambda l, j: (l, 0, j))],
        out_specs=pl.BlockSpec((None, rows, tn), lambda l, j: (l, 0, j)),
        out_shape=jax.ShapeDtypeStruct((depth, rows, n), F32),
        compiler_params=_params("parallel", "parallel"),
        name="ada_params",
    )(cond, w_ada, b_ada.reshape(depth, 1, n))


def _rope128(x, cos, sin_lo, sin_hi):
    return x * cos + pltpu.roll(x, LANES - MLA_ROPE // 2, 1) * sin_lo + pltpu.roll(x, MLA_ROPE // 2, 1) * sin_hi


def _head_rms128(x, n_valid):
    ss = jnp.sum(x * x, axis=-1, keepdims=True) * (1.0 / n_valid)
    return x * lax.rsqrt(ss + EPS)


def _pair_rms64(x, lane_lo):
    xx = x * x
    ss_lo = jnp.sum(jnp.where(lane_lo, xx, 0.0), axis=-1, keepdims=True)
    ss_hi = jnp.sum(jnp.where(lane_lo, 0.0, xx), axis=-1, keepdims=True)
    inv = jnp.where(lane_lo, lax.rsqrt(ss_lo * (1.0 / HEAD_DIM) + EPS), lax.rsqrt(ss_hi * (1.0 / HEAD_DIM) + EPS))
    return x * inv


def _attn_in_kernel(x_ref, g1_ref, sh_ref, sc_ref, w_in_ref, gqa_ref, wq_ref, gkva_ref, wk_ref, wv_ref,
                    gq_ref, gk_ref, gnq_ref, gnk_ref, cos_ref, slo_ref, shi_ref, *out_refs, latent):
    if latent:
        qm_ref, km_ref, vm_ref, qn_ref, kn_ref, vn_ref = out_refs
    else:
        km_ref, vm_ref, kn_ref, vn_ref = out_refs
    a = _rms_mod(x_ref[...], g1_ref[...], sh_ref[...], sc_ref[...]).astype(BF16)
    z = _mm(a, w_in_ref[...])
    c0, c1, c2 = 384, 384 + 256, 384 + 256 + LANES
    cq, ckv, kr = z[:, :c0], z[:, c0:c1], z[:, c1:c2]
    naq, nak, nav = z[:, c2:c2 + 512], z[:, c2 + 512:c2 + 1024], z[:, c2 + 1024:c2 + 1536]

    def rms(v, g):
        return v * lax.rsqrt(jnp.mean(v * v, axis=-1, keepdims=True) + EPS) * g

    ckv_n = rms(ckv, gkva_ref[...]).astype(BF16)
    kfull = _mm(ckv_n, wk_ref[...])
    vm_ref[...] = _mm(ckv_n, wv_ref[...]).astype(BF16)
    if latent:
        qfull = _mm(rms(cq, gqa_ref[...]).astype(BF16), wq_ref[...])
        cos, slo, shi = cos_ref[...], slo_ref[...], shi_ref[...]
    gq, gk = gq_ref[...], gk_ref[...]
    for h in range(N_HEADS):
        sl = slice(h * LANES, (h + 1) * LANES)
        kh = _head_rms128(kfull[:, sl] + kr, MLA_QK) * gk
        if latent:
            kh = _rope128(kh, cos, slo, shi)
            qh = _rope128(_head_rms128(qfull[:, sl], MLA_QK) * gq, cos, slo, shi)
            qm_ref[:, sl] = qh.astype(BF16)
        km_ref[:, sl] = kh.astype(BF16)
    lane_lo = lax.broadcasted_iota(jnp.int32, (1, LANES), 1) < HEAD_DIM
    gnq, gnk = gnq_ref[...], gnk_ref[...]
    for p in range(N_HEADS // 2):
        sl = slice(p * LANES, (p + 1) * LANES)
        kn_ref[:, sl] = (_pair_rms64(nak[:, sl], lane_lo) * gnk).astype(BF16)
        if latent:
            qn_ref[:, sl] = (_pair_rms64(naq[:, sl], lane_lo) * gnq).astype(BF16)
    vn_ref[...] = nav.astype(BF16)


def attn_inputs(x2, batch_of_tile, pos_of_tile, tm, g1, sh, sc, wts, rope_tabs, *, latent):
    t, d = x2.shape
    w_in, gqa, wq, gkva, wk, wv, gq, gk, gnq, gnk = wts
    full = lambda arr: pl.BlockSpec(arr.shape, lambda i: (0,) * arr.ndim)
    row = pl.BlockSpec((None, 1, d), lambda i: (batch_of_tile(i), 0, 0))
    tab = pl.BlockSpec((tm, LANES), lambda i: (pos_of_tile(i), 0))
    wide = jax.ShapeDtypeStruct((t, N_HEADS * LANES), BF16)
    half = jax.ShapeDtypeStruct((t, N_HEADS * HEAD_DIM), BF16)
    o_wide = pl.BlockSpec((tm, N_HEADS * LANES), lambda i: (i, 0))
    o_half = pl.BlockSpec((tm, N_HEADS * HEAD_DIM), lambda i: (i, 0))
    if latent:
        out_shape, out_specs = (wide, wide, half, half, half, half), (o_wide, o_wide, o_half, o_half, o_half, o_half)
    else:
        out_shape, out_specs = (wide, half, half, half), (o_wide, o_half, o_half, o_half)
    return pl.pallas_call(
        functools.partial(_attn_in_kernel, latent=latent),
        grid=(t // tm,),
        in_specs=[pl.BlockSpec((tm, d), lambda i: (i, 0)), full(g1), row, row,
                  full(w_in), full(gqa), full(wq), full(gkva), full(wk), full(wv),
                  full(gq), full(gk), full(gnq), full(gnk), tab, tab, tab],
        out_specs=out_specs,
        out_shape=out_shape,
        compiler_params=_params("parallel"),
        name="attn_inputs_lat" if latent else "attn_inputs_ctx",
    )(x2, g1, sh, sc, w_in, gqa, wq, gkva, wk, wv, gq, gk, gnq, gnk, *rope_tabs)


def _softmax_pv(s_list, v_list):
    m = functools.reduce(jnp.maximum, [jnp.max(s, axis=-1, keepdims=True) for s in s_list])
    ps = [jnp.exp(s - m) for s in s_list]
    denom = functools.reduce(jnp.add, [jnp.sum(p, axis=-1, keepdims=True) for p in ps])
    o = functools.reduce(jnp.add, [_mm(p.astype(BF16), v) for p, v in zip(ps, v_list)])
    return o / denom


def _mla_kernel(q_ref, k_ref, v_ref, kc_ref, vc_ref, o_ref):
    lane_lo = lax.broadcasted_iota(jnp.int32, (1, LANES), 1) < HEAD_DIM
    for p in range(N_HEADS // 2):
        vsl = slice(p * LANES, (p + 1) * LANES)
        outs = []
        for h in (2 * p, 2 * p + 1):
            sl = slice(h * LANES, (h + 1) * LANES)
            q = q_ref[:, sl]
            outs.append(_softmax_pv([_mm_nt(q, k_ref[:, sl]), _mm_nt(q, kc_ref[:, sl])],
                                    [v_ref[:, vsl], vc_ref[:, vsl]]))
        o_ref[:, vsl] = jnp.where(lane_lo, outs[0], outs[1]).astype(BF16)


def mla_attention(qm, km, vm, kmc, vmc, *, tq=512):
    b, s, dq = qm.shape
    c, dv = kmc.shape[1], vm.shape[2]
    return pl.pallas_call(
        _mla_kernel,
        grid=(b, s // tq),
        in_specs=[pl.BlockSpec((None, tq, dq), lambda i, j: (i, j, 0)),
                  pl.BlockSpec((None, s, dq), lambda i, j: (i, 0, 0)),
                  pl.BlockSpec((None, s, dv), lambda i, j: (i, 0, 0)),
                  pl.BlockSpec((None, c, dq), lambda i, j: (i, 0, 0)),
                  pl.BlockSpec((None, c, dv), lambda i, j: (i, 0, 0))],
        out_specs=pl.BlockSpec((None, tq, dv), lambda i, j: (i, j, 0)),
        out_shape=jax.ShapeDtypeStruct((b, s, dv), BF16),
        compiler_params=_params("parallel", "arbitrary"),
        name="mla_attention",
    )(qm, km, vm, kmc, vmc)


def _na_row0(r, n_rows):
    return jnp.clip(r - NA_WIN_ROWS // 2, 0, n_rows - NA_WIN_ROWS)


def _na_kernel(q_ref, k_ref, v_ref, kc_ref, vc_ref, bias_ref, o_ref, *, n_rows, rows_per_step):
    band = NA_WIN_ROWS * GRID_W
    lane_lo = lax.broadcasted_iota(jnp.int32, (1, LANES), 1) < HEAD_DIM
    for rr in range(rows_per_step):
        r = pl.program_id(1) * rows_per_step + rr
        row0 = _na_row0(r, n_rows)
        start = pl.multiple_of(row0 * GRID_W, GRID_W)
        cls = r - row0
        qrows = slice(rr * GRID_W, (rr + 1) * GRID_W)
        for p in range(N_HEADS // 2):
            sl = slice(p * LANES, (p + 1) * LANES)
            qp = q_ref[qrows, sl]
            kp, vp = k_ref[pl.ds(start, band), sl], v_ref[pl.ds(start, band), sl]
            kcp, vcp = kc_ref[:, sl], vc_ref[:, sl]
            zero = jnp.zeros_like(qp)
            q2 = jnp.concatenate([jnp.where(lane_lo, qp, zero), jnp.where(lane_lo, zero, qp)], axis=0)
            bias2 = jnp.concatenate([bias_ref[cls, 2 * p], bias_ref[cls, 2 * p + 1]], axis=0)
            o2 = _softmax_pv([_mm_nt(q2, kp) + bias2, _mm_nt(q2, kcp)], [vp, vcp])
            o_ref[qrows, sl] = jnp.where(lane_lo, o2[:GRID_W], o2[GRID_W:]).astype(BF16)


def na_attention(qn, kn, vn, knc, vnc, bias, *, rows_per_step=4):
    b, s, dh = qn.shape
    c = knc.shape[1]
    n_rows = s // GRID_W
    tq = rows_per_step * GRID_W
    whole = lambda n: pl.BlockSpec((None, n, dh), lambda i, r: (i, 0, 0))
    return pl.pallas_call(
        functools.partial(_na_kernel, n_rows=n_rows, rows_per_step=rows_per_step),
        grid=(b, n_rows // rows_per_step),
        in_specs=[pl.BlockSpec((None, tq, dh), lambda i, r: (i, r, 0)),
                  whole(s), whole(s), whole(c), whole(c),
                  pl.BlockSpec(bias.shape, lambda i, r: (0, 0, 0, 0))],
        out_specs=pl.BlockSpec((None, tq, dh), lambda i, r: (i, r, 0)),
        out_shape=jax.ShapeDtypeStruct((b, s, dh), BF16),
        compiler_params=_params("parallel", "arbitrary"),
        name="na_attention",
    )(qn, kn, vn, knc, vnc, bias)


def na_bias_table(rpb):
    n_heads, n_rel_r, n_rel_c = rpb.shape
    q = np.arange(GRID_W)
    c0 = np.clip(q - NA_WIN_COLS // 2, 0, GRID_W - NA_WIN_COLS)
    kc = np.arange(GRID_W)
    inside = (kc[None, :] >= c0[:, None]) & (kc[None, :] < c0[:, None] + NA_WIN_COLS)
    rel_c = kc[None, :] - q[:, None] + NA_WIN_COLS - 1
    onehot = ((rel_c[None] == np.arange(n_rel_c)[:, None, None]) & inside[None]).astype(np.float32)
    toep = jnp.einsum("hrc,cqk->hrqk", rpb, jnp.asarray(onehot), precision=lax.Precision.HIGHEST)
    toep = jnp.where(inside[None, None], toep, NEG_BIG)
    per_cls = [toep[:, NA_WIN_ROWS - 1 - cls:2 * NA_WIN_ROWS - 1 - cls] for cls in range(NA_WIN_ROWS)]
    tab = jnp.stack(per_cls)
    return tab.transpose(0, 1, 3, 2, 4).reshape(NA_WIN_ROWS, n_heads, GRID_W, NA_WIN_ROWS * GRID_W)


def _out_proj_kernel(om_ref, on_ref, wa_ref, wb_ref, h_ref, gate_ref, out_ref):
    y = _mm(om_ref[...], wa_ref[...]) + _mm(on_ref[...], wb_ref[...])
    out_ref[...] = h_ref[...] + gate_ref[...] * y


def out_proj(om, on, wa, wb, h2, gate, tiles_per_batch, *, tm=512):
    t, d = h2.shape
    k = om.shape[1]
    return pl.pallas_call(
        _out_proj_kernel,
        grid=(t // tm,),
        in_specs=[pl.BlockSpec((tm, k), lambda i: (i, 0)), pl.BlockSpec((tm, k), lambda i: (i, 0)),
                  pl.BlockSpec((k, d), lambda i: (0, 0)), pl.BlockSpec((k, d), lambda i: (0, 0)),
                  pl.BlockSpec((tm, d), lambda i: (i, 0)),
                  pl.BlockSpec((None, 1, d), lambda i: (i // tiles_per_batch(tm), 0, 0))],
        out_specs=pl.BlockSpec((tm, d), lambda i: (i, 0)),
        out_shape=jax.ShapeDtypeStruct((t, d), F32),
        compiler_params=_params("parallel"),
        name="out_proj",
    )(om, on, wa, wb, h2, gate)


def _sort16_comparators():
    comps = []

    def merge(lo, n, r):
        step = r * 2
        if step < n:
            merge(lo, n, step)
            merge(lo + r, n, step)
            comps.extend((i, i + r) for i in range(lo + r, lo + n - r, step))
        else:
            comps.append((lo, lo + r))

    def sort(lo, n):
        if n > 1:
            sort(lo, n // 2)
            sort(lo + n // 2, n // 2)
            merge(lo, n, 1)

    sort(0, PEER_TOPK)
    return comps


def _cmp_exchange(x, i, j):
    a, b = x[i], x[j]
    if a is None:
        x[i], x[j] = b, None
    elif b is not None:
        x[i], x[j] = jnp.maximum(a, b), jnp.minimum(a, b)


def _top16_of_slabs(slabs):
    x = list(slabs) + [None] * (PEER_TOPK - len(slabs))
    for i, j in _sort16_comparators():
        _cmp_exchange(x, i, j)
    for shift in (4, 2, 1):
        y = [None if v is None else pltpu.roll(v, shift, 0) for v in x]
        z = []
        for v in range(PEER_TOPK):
            a, b = x[v], y[PEER_TOPK - 1 - v]
            z.append(b if a is None else a if b is None else jnp.maximum(a, b))
        d = PEER_TOPK // 2
        while d:
            for i in range(PEER_TOPK):
                if i & d == 0:
                    _cmp_exchange(z, i, i + d)
            d //= 2
        x = z
    return x


def _on_sublanes(slabs, sub):
    out = slabs[-1]
    for s in range(len(slabs) - 2, -1, -1):
        out = jnp.where(sub == s, slabs[s], out)
    return out


def _peer_router_kernel(x_ref, g_ref, sh_ref, sc_ref, wq_ref, keys_ref, ht_ref, rb_ref, eb_ref, ns_ref, ea_ref):
    a = _rms_mod(x_ref[...], g_ref[...], sh_ref[...], sc_ref[...])
    at = a.T.astype(BF16)
    ht_ref[...] = at
    qt = _mm(wq_ref[...], at).astype(BF16)
    k = PEER_TOPK
    n_slabs = PEER_N_KEYS // SUBLANES
    sub = lax.broadcasted_iota(jnp.int32, (SUBLANES, 1), 0)
    slab = lambda s, v: s[v * SUBLANES:(v + 1) * SUBLANES]
    for h in range(PEER_HEADS):
        sa = _mm(keys_ref[2 * h], qt[(2 * h) * LANES:(2 * h + 1) * LANES])
        sb = _mm(keys_ref[2 * h + 1], qt[(2 * h + 1) * LANES:(2 * h + 2) * LANES])
        top_a = _top16_of_slabs([slab(sa, v) for v in range(n_slabs)])
        top_b = _top16_of_slabs([slab(sb, v) for v in range(n_slabs)])
        a_lo, a_hi = _on_sublanes(top_a[:SUBLANES], sub), _on_sublanes(top_a[SUBLANES:], sub)
        b_hi = _on_sublanes(top_b[SUBLANES:], sub)
        cands = [a_lo + top_b[0], a_hi + top_b[0]]
        cands += [jnp.where(sub < k // (rj + 1), a_lo + top_b[rj], NEG_BIG) for rj in range(1, SUBLANES)]
        cands.append(top_a[0] + b_hi)
        best = _top16_of_slabs(cands)
        tau, m = best[k - 1], best[0]
        z = functools.reduce(jnp.add, [jnp.exp(c - m) for c in best])
        n_sel = [jnp.zeros_like(slab(sa, 0)) for _ in range(n_slabs)]
        for rj in range(k):
            if rj == 0:
                ok = jnp.minimum(jnp.where(cands[0] >= tau, a_lo, -NEG_BIG), jnp.where(cands[1] >= tau, a_hi, -NEG_BIG))
            elif rj < SUBLANES:
                ok = jnp.where(cands[rj + 1] >= tau, a_lo, -NEG_BIG)
            else:
                ok = jnp.where(top_a[0] + top_b[rj] >= tau, top_a[0], -NEG_BIG)
            if rj < SUBLANES:
                for shift in (4, 2, 1):
                    ok = jnp.minimum(ok, pltpu.roll(ok, shift, 0))
            n_sel = [n + jnp.where(slab(sa, v) >= ok, 1.0, 0.0) for v, n in enumerate(n_sel)]
        rank_b = [jnp.zeros_like(slab(sb, 0)) for _ in range(n_slabs)]
        for r in range(k):
            rank_b = [rk + jnp.where(top_b[r] > slab(sb, v), 1.0, 0.0) for v, rk in enumerate(rank_b)]
        inv_z = 1.0 / z
        for v in range(n_slabs):
            rows = slice(v * SUBLANES, (v + 1) * SUBLANES)
            rb_ref[h, rows, :] = rank_b[v]
            eb_ref[h, rows, :] = jnp.exp(slab(sb, v) - top_b[0]) * inv_z
            ns_ref[h, rows, :] = n_sel[v]
            ea_ref[h, rows, :] = jnp.exp(slab(sa, v) - top_a[0])


def peer_router(h2, g, sh, sc, wq_t, keys, tiles_per_batch, *, tt=512):
    t, d = h2.shape
    tab = lambda dtype: jax.ShapeDtypeStruct((PEER_HEADS, PEER_N_KEYS, t), dtype)
    tab_spec = pl.BlockSpec((PEER_HEADS, PEER_N_KEYS, tt), lambda i: (0, 0, i))
    row = pl.BlockSpec((None, 1, d), lambda i: (i // tiles_per_batch(tt), 0, 0))
    return pl.pallas_call(
        _peer_router_kernel,
        grid=(t // tt,),
        in_specs=[pl.BlockSpec((tt, d), lambda i: (i, 0)), pl.BlockSpec((1, d), lambda i: (0, 0)), row, row,
                  pl.BlockSpec(wq_t.shape, lambda i: (0, 0)), pl.BlockSpec(keys.shape, lambda i: (0, 0, 0))],
        out_specs=(pl.BlockSpec((d, tt), lambda i: (0, i)), tab_spec, tab_spec, tab_spec, tab_spec),
        out_shape=(jax.ShapeDtypeStruct((d, t), BF16), tab(F32), tab(F32), tab(F32), tab(F32)),
        compiler_params=_params("parallel"),
        name="peer_router",
    )(h2, g, sh, sc, wq_t, keys)


def _gelu_exact(x):
    return 0.5 * x * (1.0 + lax.erf(x * np.float32(np.sqrt(0.5))))


def _peer_expert_kernel(ht_ref, u_ref, vt_ref, rb_ref, eb_ref, ns_ref, ea_ref, h_ref, gate_ref, out_ref,
                        acc_ref, w_ref, *, rows_per_block):
    e = pl.program_id(1)
    tt = acc_ref.shape[1]
    n_chunks = rows_per_block // 2
    rows_c = 2 * PEER_N_KEYS
    jt = 2 * SUBLANES

    def score_chunk(c):
        r0 = c * rows_c
        act = _gelu_exact(_mm(u_ref[r0:r0 + rows_c, :], ht_ref[...]))
        for lg in range(tt // LANES):
            lanes = slice(lg * LANES, (lg + 1) * LANES)
            for n in range(2):
                il = 2 * c + n
                rows = [(jnp.maximum(jnp.broadcast_to(ns_ref[h, il:il + 1, lanes], (SUBLANES, LANES)), 0.0),
                         jnp.maximum(jnp.broadcast_to(ea_ref[h, il:il + 1, lanes], (SUBLANES, LANES)), 0.0))
                        for h in range(PEER_HEADS)]
                for j0 in range(0, PEER_N_KEYS, jt):
                    halves = []
                    for k in range(jt // SUBLANES):
                        js = slice(j0 + k * SUBLANES, j0 + (k + 1) * SUBLANES)
                        g = None
                        for h, (n_i, ea) in enumerate(rows):
                            term = jnp.where(rb_ref[h, js, lanes] < n_i, eb_ref[h, js, lanes], 0.0) * ea
                            g = term if g is None else g + term
                        off = n * PEER_N_KEYS + j0 + k * SUBLANES
                        halves.append(g * act[off:off + SUBLANES, lanes])
                    off = n * PEER_N_KEYS + j0
                    w_ref[r0 + off:r0 + off + jt, lanes] = jnp.concatenate(halves, axis=0).astype(BF16)

    @pl.when(e == 0)
    def _():
        acc_ref[...] = jnp.zeros_like(acc_ref)

    for c in range(n_chunks):
        score_chunk(c)
    acc_ref[...] += _mm(vt_ref[...], w_ref[...])

    @pl.when(e == pl.num_programs(1) - 1)
    def _():
        out_ref[...] = h_ref[...] + gate_ref[...] * acc_ref[...].T


def peer_experts(ht, u_bf, vt_bf, rb, eb, ns, ea, h2, gate, tiles_per_batch, *, tt=512, te=2048):
    t, d = h2.shape
    n_blocks = u_bf.shape[0] // te
    rows = te // PEER_N_KEYS
    full_tab = pl.BlockSpec((PEER_HEADS, PEER_N_KEYS, tt), lambda i, e: (0, 0, i))
    row_tab = pl.BlockSpec((PEER_HEADS, rows, tt), lambda i, e: (0, e, i))
    return pl.pallas_call(
        functools.partial(_peer_expert_kernel, rows_per_block=rows),
        grid=(t // tt, n_blocks),
        in_specs=[pl.BlockSpec((d, tt), lambda i, e: (0, i)),
                  pl.BlockSpec((te, d), lambda i, e: (e, 0)),
                  pl.BlockSpec((d, te), lambda i, e: (0, e)),
                  full_tab, full_tab, row_tab, row_tab,
                  pl.BlockSpec((tt, d), lambda i, e: (i, 0)),
                  pl.BlockSpec((None, 1, d), lambda i, e: (i // tiles_per_batch(tt), 0, 0))],
        out_specs=pl.BlockSpec((tt, d), lambda i, e: (i, 0)),
        out_shape=jax.ShapeDtypeStruct((t, d), F32),
        scratch_shapes=[pltpu.VMEM((d, tt), F32), pltpu.VMEM((te, tt), BF16)],
        compiler_params=_params("parallel", "arbitrary"),
        name="peer_experts",
    )(ht, u_bf, vt_bf, rb, eb, ns, ea, h2, gate)


def peer_layer(h2, g, sh, sc, gate, w_query, sub_keys, u_tab, v_tab, tiles_per_batch):
    wq_t = w_query.T.astype(BF16)
    keys = sub_keys.reshape(PEER_HEADS * 2, PEER_N_KEYS, -1).astype(BF16)
    ht, sb, eb, th, ea = peer_router(h2, g, sh, sc, wq_t, keys, tiles_per_batch)
    return peer_experts(ht, u_tab.astype(BF16), v_tab.T.astype(BF16), sb, eb, th, ea, h2, gate, tiles_per_batch)


def _glu_kernel(x_ref, g_ref, sh_ref, sc_ref, w_ref, b_ref, out_ref):
    a = _rms_mod(x_ref[...], g_ref[...], sh_ref[...], sc_ref[...]).astype(BF16)
    y = _mm(a, w_ref[...]) + b_ref[...]
    d = out_ref.shape[-1]
    out_ref[...] = y[:, :d] * jax.nn.sigmoid(y[:, d:])


def conv_glu(h2, g, sh, sc, w_pw1, b_pw1, tiles_per_batch, *, tm=512):
    t, d = h2.shape
    row = pl.BlockSpec((None, 1, d), lambda i: (i // tiles_per_batch(tm), 0, 0))
    return pl.pallas_call(
        _glu_kernel,
        grid=(t // tm,),
        in_specs=[pl.BlockSpec((tm, d), lambda i: (i, 0)), pl.BlockSpec((1, d), lambda i: (0, 0)), row, row,
                  pl.BlockSpec((d, 2 * d), lambda i: (0, 0)), pl.BlockSpec((1, 2 * d), lambda i: (0, 0))],
        out_specs=pl.BlockSpec((tm, d), lambda i: (i, 0)),
        out_shape=jax.ShapeDtypeStruct((t, d), F32),
        compiler_params=_params("parallel"),
        name="conv_glu",
    )(h2, g, sh, sc, w_pw1, b_pw1)


def _conv_tail_kernel(prev_ref, cur_ref, next_ref, wdw_ref, bdw_ref, gln_ref, bln_ref, w2_ref, b2_ref,
                      h_ref, gate_ref, out_ref, win_ref, shift_ref, *, ts):
    s = pl.program_id(1)
    halo = CONV_HALO
    win_ref[0:halo, :] = jnp.where(s > 0, prev_ref[...], 0.0)
    win_ref[halo:halo + ts, :] = cur_ref[...]
    win_ref[halo + ts:, :] = jnp.where(s < pl.num_programs(1) - 1, next_ref[...], 0.0)
    acc = jnp.zeros((ts, cur_ref.shape[-1]), F32) + bdw_ref[...]
    base = halo - CONV_WIDTH // 2
    span = shift_ref.shape[0]
    for r in range(SUBLANES):
        src = win_ref
        if r:
            shift_ref[...] = win_ref[r:r + span, :]
            src = shift_ref
        for a in range(0, span - ts + 1, SUBLANES):
            k = a + r - base
            if 0 <= k < CONV_WIDTH:
                acc = acc + src[a:a + ts, :] * wdw_ref[k:k + 1, :]
    mu = jnp.mean(acc, axis=-1, keepdims=True)
    cen = acc - mu
    var = jnp.mean(cen * cen, axis=-1, keepdims=True)
    y = cen * lax.rsqrt(var + EPS) * gln_ref[...] + bln_ref[...]
    y = y * jax.nn.sigmoid(y)
    y = _mm(y.astype(BF16), w2_ref[...]) + b2_ref[...]
    out_ref[...] = h_ref[...] + gate_ref[...] * y


def conv_tail(a3, w_dw, b_dw, g_ln, b_ln, w_pw2, b_pw2, h3, gate, *, ts=256):
    b, s, d = a3.shape
    hb = ts // CONV_HALO
    n_halo = s // CONV_HALO
    vec = pl.BlockSpec((1, d), lambda i, j: (0, 0))
    return pl.pallas_call(
        functools.partial(_conv_tail_kernel, ts=ts),
        grid=(b, s // ts),
        in_specs=[pl.BlockSpec((None, CONV_HALO, d), lambda i, j: (i, jnp.maximum(j * hb - 1, 0), 0)),
                  pl.BlockSpec((None, ts, d), lambda i, j: (i, j, 0)),
                  pl.BlockSpec((None, CONV_HALO, d), lambda i, j: (i, jnp.minimum((j + 1) * hb, n_halo - 1), 0)),
                  pl.BlockSpec((CONV_WIDTH, d), lambda i, j: (0, 0)), vec, vec, vec,
                  pl.BlockSpec((d, d), lambda i, j: (0, 0)), vec,
                  pl.BlockSpec((None, ts, d), lambda i, j: (i, j, 0)),
                  pl.BlockSpec((None, 1, d), lambda i, j: (i, 0, 0))],
        out_specs=pl.BlockSpec((None, ts, d), lambda i, j: (i, j, 0)),
        out_shape=jax.ShapeDtypeStruct((b, s, d), F32),
        scratch_shapes=[pltpu.VMEM((ts + 2 * CONV_HALO, d), F32),
                        pltpu.VMEM((ts + 2 * CONV_HALO - SUBLANES, d), F32)],
        compiler_params=_params("parallel", "arbitrary"),
        name="conv_tail",
    )(a3, a3, a3, w_dw, b_dw, g_ln, b_ln, w_pw2, b_pw2, h3, gate)


def _pad_heads(w, width):
    k = w.shape[0]
    w = w.reshape(k, N_HEADS, width)
    return jnp.pad(w, ((0, 0), (0, 0), (0, LANES - width))).reshape(k, N_HEADS * LANES)


def _attn_weights(w_in, g_qa, w_q_up, g_kva, w_kv_up, g_q, g_k, na_g_q, na_g_k):
    a, b, c = 384, 384 + 256, 384 + 256 + MLA_ROPE
    kr_cols = jnp.pad(w_in[:, b:c], ((0, 0), (MLA_NOPE, LANES - MLA_QK)))
    w_in_r = jnp.concatenate([w_in[:, :a], w_in[:, a:b], kr_cols, w_in[:, c:]], axis=1).astype(BF16)
    wq = _pad_heads(w_q_up, MLA_QK).astype(BF16)
    kv = w_kv_up.reshape(-1, N_HEADS, MLA_NOPE + HEAD_DIM)
    wk = _pad_heads(kv[:, :, :MLA_NOPE].reshape(-1, N_HEADS * MLA_NOPE), MLA_NOPE).astype(BF16)
    wv = kv[:, :, MLA_NOPE:].reshape(-1, N_HEADS * HEAD_DIM).astype(BF16)
    pad_g = lambda g: jnp.pad(g, (0, LANES - MLA_QK)).reshape(1, LANES)
    gq = pad_g(g_q) * (MLA_QK ** -0.5)
    gk = pad_g(g_k)
    gnq = jnp.tile(na_g_q, 2).reshape(1, LANES) * (HEAD_DIM ** -0.5)
    gnk = jnp.tile(na_g_k, 2).reshape(1, LANES)
    return (w_in_r, g_qa.reshape(1, -1), wq, g_kva.reshape(1, -1), wk, wv, gq, gk, gnq, gnk)


def _rope_tables(s):
    t = jnp.arange(s)
    row = (t // GRID_W).astype(F32)
    col = (t % GRID_W).astype(F32)
    n_freq = MLA_ROPE // 4
    inv_freq = 1.0 / (ROPE_THETA ** (jnp.arange(n_freq, dtype=F32) / n_freq))
    ang = jnp.concatenate([row[:, None] * inv_freq, col[:, None] * inv_freq], axis=-1)
    cos, sin = jnp.cos(ang), jnp.sin(ang)
    half = MLA_ROPE // 2
    ones = jnp.ones((s, MLA_NOPE), F32)
    z = lambda n: jnp.zeros((s, n), F32)
    cos_t = jnp.concatenate([ones, cos, cos, z(LANES - MLA_QK)], axis=1)
    sin_lo = jnp.concatenate([z(MLA_NOPE), -sin, z(half), z(LANES - MLA_QK)], axis=1)
    sin_hi = jnp.concatenate([z(MLA_NOPE), z(half), sin, z(LANES - MLA_QK)], axis=1)
    return cos_t, sin_lo, sin_hi


def kernel(x, c, ctx, c_ctx, w_ada, b_ada, g_norm1, g_norm2, attn_w_in, mla_g_qa, mla_w_q_up, mla_g_kva, mla_w_kv_up, mla_g_q, mla_g_k, na_g_q, na_g_k, na_rpb, attn_w_out, conv_w_pw1, conv_b_pw1, conv_w_dw, conv_b_dw, conv_g_ln, conv_b_ln, conv_w_pw2, conv_b_pw2, peer_w_query, peer_sub_keys, peer_u, peer_v):
    b, s, d = x.shape
    n_ctx = ctx.shape[1]
    t = b * s
    depth = w_ada.shape[0]
    assert depth == 2
    tiles_per_batch = lambda tile: s // tile

    cond = jnp.zeros((16, d), F32).at[:b].set(c).at[b].set(c_ctx)
    mod = ada_params(cond, w_ada, b_ada).reshape(depth, 16, 6, 1, d)
    mods = [[mod[l, :, k] for k in range(6)] for l in range(depth)]

    h2 = x.reshape(t, d)
    for l in range(depth):
        sh1, sc1, g1, sh2, sc2, g2 = mods[l]
        i = l // 2
        gn1 = g_norm1[l].reshape(1, d)
        if l % 2 == 0:
            wts = _attn_weights(attn_w_in[i], mla_g_qa[i], mla_w_q_up[i], mla_g_kva[i], mla_w_kv_up[i],
                                mla_g_q[i], mla_g_k[i], na_g_q[i], na_g_k[i])
            tabs = _rope_tables(s)
            tm = 256
            qm, km, vm, qn, kn, vn = attn_inputs(
                h2, lambda j: j // (s // tm), lambda j: j % (s // tm), tm, gn1, sh1, sc1, wts, tabs, latent=True)
            tc = n_ctx
            kmc, vmc, knc, vnc = attn_inputs(
                ctx.reshape(b * n_ctx, d), lambda j: b, lambda j: 0, tc, gn1, sh1, sc1, wts,
                tuple(tb[:tc] for tb in tabs), latent=False)
            r3 = lambda arr, n: arr.reshape(b, n, arr.shape[-1])
            om = mla_attention(r3(qm, s), r3(km, s), r3(vm, s), r3(kmc, n_ctx), r3(vmc, n_ctx))
            on = na_attention(r3(qn, s), r3(kn, s), r3(vn, s), r3(knc, n_ctx), r3(vnc, n_ctx),
                              na_bias_table(na_rpb[i]))
            w_out = attn_w_out[i].astype(BF16)
            half = w_out.shape[0] // 2
            h2 = out_proj(om.reshape(t, -1), on.reshape(t, -1), w_out[:half], w_out[half:], h2, g1,
                          tiles_per_batch)
        else:
            a2 = conv_glu(h2, gn1, sh1, sc1, conv_w_pw1[i].astype(BF16), conv_b_pw1[i].reshape(1, -1),
                          tiles_per_batch)
            h2 = conv_tail(a2.reshape(b, s, d), conv_w_dw[i], conv_b_dw[i].reshape(1, d),
                           conv_g_ln[i].reshape(1, d), conv_b_ln[i].reshape(1, d),
                           conv_w_pw2[i].astype(BF16), conv_b_pw2[i].reshape(1, d),
                           h2.reshape(b, s, d), g1).reshape(t, d)
        h2 = peer_layer(h2, g_norm2[l].reshape(1, d), sh2, sc2, g2, peer_w_query[l], peer_sub_keys[l],
                        peer_u[l], peer_v[l], tiles_per_batch)
    return h2.reshape(b, s, d)
```

```python
import functools

import jax
import jax.numpy as jnp
import numpy as np
from jax import lax
from jax.experimental import pallas as pl
from jax.experimental.pallas import tpu as pltpu

F32 = jnp.float32
BF16 = jnp.bfloat16

LANES = 128
SUBLANES = 8
EPS = 1e-6
NEG_BIG = -1e30
VMEM_LIMIT = 56 * 1024 * 1024

GRID_W = 64
HEAD_DIM = 64
N_HEADS = 8
MLA_NOPE = 64
MLA_ROPE = 32
MLA_QK = MLA_NOPE + MLA_ROPE
NA_WIN_ROWS = 8
NA_WIN_COLS = 16
ROPE_THETA = 10000.0
CONV_WIDTH = 31
CONV_HALO = 16
PEER_HEADS = 8
PEER_N_KEYS = 128
PEER_TOPK = 16

NT_DIMS = (((1,), (1,)), ((), ()))


def _params(*sem):
    return pltpu.CompilerParams(dimension_semantics=sem, vmem_limit_bytes=VMEM_LIMIT)


def _mm(a, b):
    return jnp.dot(a, b, preferred_element_type=F32)


def _mm_nt(a, b):
    return lax.dot_general(a, b, NT_DIMS, preferred_element_type=F32)


def _rms_mod(x, g, shift, scale):
    inv = lax.rsqrt(jnp.mean(x * x, axis=-1, keepdims=True) + EPS)
    return (x * inv * g) * (1.0 + scale) + shift


def _ada_kernel(cond_ref, w_ref, b_ref, out_ref):
    cond = cond_ref[...]
    a = cond * jax.nn.sigmoid(cond)
    out_ref[...] = jnp.dot(a, w_ref[...], preferred_element_type=F32,
                           precision=lax.Precision.HIGHEST) + b_ref[...]


def ada_params(cond, w_ada, b_ada, *, tn=1536):
    depth, d, n = w_ada.shape
    rows = cond.shape[0]
    return pl.pallas_call(
        _ada_kernel,
        grid=(depth, n // tn),
        in_specs=[pl.BlockSpec((rows, d), lambda l, j: (0, 0)),
                  pl.BlockSpec((None, d, tn), lambda l, j: (l, 0, j)),
                  pl.BlockSpec((None, 1, tn), lambda l, j: (l, 0, j))],
        out_specs=pl.BlockSpec((None, rows, tn), lambda l, j: (l, 0, j)),
        out_shape=jax.ShapeDtypeStruct((depth, rows, n), F32),
        compiler_params=_params("parallel", "parallel"),
        name="ada_params",
    )(cond, w_ada, b_ada.reshape(depth, 1, n))


def _rope128(x, cos, sin_lo, sin_hi):
    return x * cos + pltpu.roll(x, LANES - MLA_ROPE // 2, 1) * sin_lo + pltpu.roll(x, MLA_ROPE // 2, 1) * sin_hi


def _head_rms128(x, n_valid):
    ss = jnp.sum(x * x, axis=-1, keepdims=True) * (1.0 / n_valid)
    return x * lax.rsqrt(ss + EPS)


def _pair_rms64(x, lane_lo):
    xx = x * x
    ss_lo = jnp.sum(jnp.where(lane_lo, xx, 0.0), axis=-1, keepdims=True)
    ss_hi = jnp.sum(jnp.where(lane_lo, 0.0, xx), axis=-1, keepdims=True)
    inv = jnp.where(lane_lo, lax.rsqrt(ss_lo * (1.0 / HEAD_DIM) + EPS), lax.rsqrt(ss_hi * (1.0 / HEAD_DIM) + EPS))
    return x * inv


def _attn_in_kernel(x_ref, g1_ref, sh_ref, sc_ref, w_in_ref, gqa_ref, wq_ref, gkva_ref, wk_ref, wv_ref,
                    gq_ref, gk_ref, gnq_ref, gnk_ref, cos_ref, slo_ref, shi_ref, *out_refs, latent):
    if latent:
        qm_ref, km_ref, vm_ref, qn_ref, kn_ref, vn_ref = out_refs
    else:
        km_ref, vm_ref, kn_ref, vn_ref = out_refs
    a = _rms_mod(x_ref[...], g1_ref[...], sh_ref[...], sc_ref[...]).astype(BF16)
    z = _mm(a, w_in_ref[...])
    c0, c1, c2 = 384, 384 + 256, 384 + 256 + LANES
    cq, ckv, kr = z[:, :c0], z[:, c0:c1], z[:, c1:c2]
    naq, nak, nav = z[:, c2:c2 + 512], z[:, c2 + 512:c2 + 1024], z[:, c2 + 1024:c2 + 1536]

    def rms(v, g):
        return v * lax.rsqrt(jnp.mean(v * v, axis=-1, keepdims=True) + EPS) * g

    ckv_n = rms(ckv, gkva_ref[...]).astype(BF16)
    kfull = _mm(ckv_n, wk_ref[...])
    vm_ref[...] = _mm(ckv_n, wv_ref[...]).astype(BF16)
    if latent:
        qfull = _mm(rms(cq, gqa_ref[...]).astype(BF16), wq_ref[...])
        cos, slo, shi = cos_ref[...], slo_ref[...], shi_ref[...]
    gq, gk = gq_ref[...], gk_ref[...]
    for h in range(N_HEADS):
        sl = slice(h * LANES, (h + 1) * LANES)
        kh = _head_rms128(kfull[:, sl] + kr, MLA_QK) * gk
        if latent:
            kh = _rope128(kh, cos, slo, shi)
            qh = _rope128(_head_rms128(qfull[:, sl], MLA_QK) * gq, cos, slo, shi)
            qm_ref[:, sl] = qh.astype(BF16)
        km_ref[:, sl] = kh.astype(BF16)
    lane_lo = lax.broadcasted_iota(jnp.int32, (1, LANES), 1) < HEAD_DIM
    gnq, gnk = gnq_ref[...], gnk_ref[...]
    for p in range(N_HEADS // 2):
        sl = slice(p * LANES, (p + 1) * LANES)
        kn_ref[:, sl] = (_pair_rms64(nak[:, sl], lane_lo) * gnk).astype(BF16)
        if latent:
            qn_ref[:, sl] = (_pair_rms64(naq[:, sl], lane_lo) * gnq).astype(BF16)
    vn_ref[...] = nav.astype(BF16)


def attn_inputs(x2, batch_of_tile, pos_of_tile, tm, g1, sh, sc, wts, rope_tabs, *, latent):
    t, d = x2.shape
    w_in, gqa, wq, gkva, wk, wv, gq, gk, gnq, gnk = wts
    full = lambda arr: pl.BlockSpec(arr.shape, lambda i: (0,) * arr.ndim)
    row = pl.BlockSpec((None, 1, d), lambda i: (batch_of_tile(i), 0, 0))
    tab = pl.BlockSpec((tm, LANES), lambda i: (pos_of_tile(i), 0))
    wide = jax.ShapeDtypeStruct((t, N_HEADS * LANES), BF16)
    half = jax.ShapeDtypeStruct((t, N_HEADS * HEAD_DIM), BF16)
    o_wide = pl.BlockSpec((tm, N_HEADS * LANES), lambda i: (i, 0))
    o_half = pl.BlockSpec((tm, N_HEADS * HEAD_DIM), lambda i: (i, 0))
    if latent:
        out_shape, out_specs = (wide, wide, half, half, half, half), (o_wide, o_wide, o_half, o_half, o_half, o_half)
    else:
        out_shape, out_specs = (wide, half, half, half), (o_wide, o_half, o_half, o_half)
    return pl.pallas_call(
        functools.partial(_attn_in_kernel, latent=latent),
        grid=(t // tm,),
        in_specs=[pl.BlockSpec((tm, d), lambda i: (i, 0)), full(g1), row, row,
                  full(w_in), full(gqa), full(wq), full(gkva), full(wk), full(wv),
                  full(gq), full(gk), full(gnq), full(gnk), tab, tab, tab],
        out_specs=out_specs,
        out_shape=out_shape,
        compiler_params=_params("parallel"),
        name="attn_inputs_lat" if latent else "attn_inputs_ctx",
    )(x2, g1, sh, sc, w_in, gqa, wq, gkva, wk, wv, gq, gk, gnq, gnk, *rope_tabs)


def _softmax_pv(s_list, v_list):
    m = functools.reduce(jnp.maximum, [jnp.max(s, axis=-1, keepdims=True) for s in s_list])
    ps = [jnp.exp(s - m) for s in s_list]
    denom = functools.reduce(jnp.add, [jnp.sum(p, axis=-1, keepdims=True) for p in ps])
    o = functools.reduce(jnp.add, [_mm(p.astype(BF16), v) for p, v in zip(ps, v_list)])
    return o / denom


def _mla_kernel(q_ref, k_ref, v_ref, kc_ref, vc_ref, o_ref):
    lane_lo = lax.broadcasted_iota(jnp.int32, (1, LANES), 1) < HEAD_DIM
    for p in range(N_HEADS // 2):
        vsl = slice(p * LANES, (p + 1) * LANES)
        outs = []
        for h in (2 * p, 2 * p + 1):
            sl = slice(h * LANES, (h + 1) * LANES)
            q = q_ref[:, sl]
            outs.append(_softmax_pv([_mm_nt(q, k_ref[:, sl]), _mm_nt(q, kc_ref[:, sl])],
                                    [v_ref[:, vsl], vc_ref[:, vsl]]))
        o_ref[:, vsl] = jnp.where(lane_lo, outs[0], outs[1]).astype(BF16)


def mla_attention(qm, km, vm, kmc, vmc, *, tq=512):
    b, s, dq = qm.shape
    c, dv = kmc.shape[1], vm.shape[2]
    return pl.pallas_call(
        _mla_kernel,
        grid=(b, s // tq),
        in_specs=[pl.BlockSpec((None, tq, dq), lambda i, j: (i, j, 0)),
                  pl.BlockSpec((None, s, dq), lambda i, j: (i, 0, 0)),
                  pl.BlockSpec((None, s, dv), lambda i, j: (i, 0, 0)),
                  pl.BlockSpec((None, c, dq), lambda i, j: (i, 0, 0)),
                  pl.BlockSpec((None, c, dv), lambda i, j: (i, 0, 0))],
        out_specs=pl.BlockSpec((None, tq, dv), lambda i, j: (i, j, 0)),
        out_shape=jax.ShapeDtypeStruct((b, s, dv), BF16),
        compiler_params=_params("parallel", "arbitrary"),
        name="mla_attention",
    )(qm, km, vm, kmc, vmc)


def _na_row0(r, n_rows):
    return jnp.clip(r - NA_WIN_ROWS // 2, 0, n_rows - NA_WIN_ROWS)


def _na_kernel(q_ref, k_ref, v_ref, kc_ref, vc_ref, bias_ref, o_ref, *, n_rows, rows_per_step):
    band = NA_WIN_ROWS * GRID_W
    lane_lo = lax.broadcasted_iota(jnp.int32, (1, LANES), 1) < HEAD_DIM
    for rr in range(rows_per_step):
        r = pl.program_id(1) * rows_per_step + rr
        row0 = _na_row0(r, n_rows)
        start = pl.multiple_of(row0 * GRID_W, GRID_W)
        cls = r - row0
        qrows = slice(rr * GRID_W, (rr + 1) * GRID_W)
        for p in range(N_HEADS // 2):
            sl = slice(p * LANES, (p + 1) * LANES)
            qp = q_ref[qrows, sl]
            kp, vp = k_ref[pl.ds(start, band), sl], v_ref[pl.ds(start, band), sl]
            kcp, vcp = kc_ref[:, sl], vc_ref[:, sl]
            zero = jnp.zeros_like(qp)
            q2 = jnp.concatenate([jnp.where(lane_lo, qp, zero), jnp.where(lane_lo, zero, qp)], axis=0)
            bias2 = jnp.concatenate([bias_ref[cls, 2 * p], bias_ref[cls, 2 * p + 1]], axis=0)
            o2 = _softmax_pv([_mm_nt(q2, kp) + bias2, _mm_nt(q2, kcp)], [vp, vcp])
            o_ref[qrows, sl] = jnp.where(lane_lo, o2[:GRID_W], o2[GRID_W:]).astype(BF16)


def na_attention(qn, kn, vn, knc, vnc, bias, *, rows_per_step=4):
    b, s, dh = qn.shape
    c = knc.shape[1]
    n_rows = s // GRID_W
    tq = rows_per_step * GRID_W
    whole = lambda n: pl.BlockSpec((None, n, dh), lambda i, r: (i, 0, 0))
    return pl.pallas_call(
        functools.partial(_na_kernel, n_rows=n_rows, rows_per_step=rows_per_step),
        grid=(b, n_rows // rows_per_step),
        in_specs=[pl.BlockSpec((None, tq, dh), lambda i, r: (i, r, 0)),
                  whole(s), whole(s), whole(c), whole(c),
                  pl.BlockSpec(bias.shape, lambda i, r: (0, 0, 0, 0))],
        out_specs=pl.BlockSpec((None, tq, dh), lambda i, r: (i, r, 0)),
        out_shape=jax.ShapeDtypeStruct((b, s, dh), BF16),
        compiler_params=_params("parallel", "arbitrary"),
        name="na_attention",
    )(qn, kn, vn, knc, vnc, bias)


def na_bias_table(rpb):
    n_heads, n_rel_r, n_rel_c = rpb.shape
    q = np.arange(GRID_W)
    c0 = np.clip(q - NA_WIN_COLS // 2, 0, GRID_W - NA_WIN_COLS)
    kc = np.arange(GRID_W)
    inside = (kc[None, :] >= c0[:, None]) & (kc[None, :] < c0[:, None] + NA_WIN_COLS)
    rel_c = kc[None, :] - q[:, None] + NA_WIN_COLS - 1
    onehot = ((rel_c[None] == np.arange(n_rel_c)[:, None, None]) & inside[None]).astype(np.float32)
    toep = jnp.einsum("hrc,cqk->hrqk", rpb, jnp.asarray(onehot), precision=lax.Precision.HIGHEST)
    toep = jnp.where(inside[None, None], toep, NEG_BIG)
    per_cls = [toep[:, NA_WIN_ROWS - 1 - cls:2 * NA_WIN_ROWS - 1 - cls] for cls in range(NA_WIN_ROWS)]
    tab = jnp.stack(per_cls)
    return tab.transpose(0, 1, 3, 2, 4).reshape(NA_WIN_ROWS, n_heads, GRID_W, NA_WIN_ROWS * GRID_W)


def _out_proj_kernel(om_ref, on_ref, wa_ref, wb_ref, h_ref, gate_ref, out_ref):
    y = _mm(om_ref[...], wa_ref[...]) + _mm(on_ref[...], wb_ref[...])
    out_ref[...] = h_ref[...] + gate_ref[...] * y


def out_proj(om, on, wa, wb, h2, gate, tiles_per_batch, *, tm=512):
    t, d = h2.shape
    k = om.shape[1]
    return pl.pallas_call(
        _out_proj_kernel,
        grid=(t // tm,),
        in_specs=[pl.BlockSpec((tm, k), lambda i: (i, 0)), pl.BlockSpec((tm, k), lambda i: (i, 0)),
                  pl.BlockSpec((k, d), lambda i: (0, 0)), pl.BlockSpec((k, d), lambda i: (0, 0)),
                  pl.BlockSpec((tm, d), lambda i: (i, 0)),
                  pl.BlockSpec((None, 1, d), lambda i: (i // tiles_per_batch(tm), 0, 0))],
        out_specs=pl.BlockSpec((tm, d), lambda i: (i, 0)),
        out_shape=jax.ShapeDtypeStruct((t, d), F32),
        compiler_params=_params("parallel"),
        name="out_proj",
    )(om, on, wa, wb, h2, gate)


def _sort16_comparators():
    comps = []

    def merge(lo, n, r):
        step = r * 2
        if step < n:
            merge(lo, n, step)
            merge(lo + r, n, step)
            comps.extend((i, i + r) for i in range(lo + r, lo + n - r, step))
        else:
            comps.append((lo, lo + r))

    def sort(lo, n):
        if n > 1:
            sort(lo, n // 2)
            sort(lo + n // 2, n // 2)
            merge(lo, n, 1)

    sort(0, PEER_TOPK)
    return comps


def _cmp_exchange(x, i, j):
    a, b = x[i], x[j]
    if a is None:
        x[i], x[j] = b, None
    elif b is not None:
        x[i], x[j] = jnp.maximum(a, b), jnp.minimum(a, b)


def _top16_of_slabs(slabs):
    x = list(slabs) + [None] * (PEER_TOPK - len(slabs))
    for i, j in _sort16_comparators():
        _cmp_exchange(x, i, j)
    for shift in (4, 2, 1):
        y = [None if v is None else pltpu.roll(v, shift, 0) for v in x]
        z = []
        for v in range(PEER_TOPK):
            a, b = x[v], y[PEER_TOPK - 1 - v]
            z.append(b if a is None else a if b is None else jnp.maximum(a, b))
        d = PEER_TOPK // 2
        while d:
            for i in range(PEER_TOPK):
                if i & d == 0:
                    _cmp_exchange(z, i, i + d)
            d //= 2
        x = z
    return x


def _on_sublanes(slabs, sub):
    out = slabs[-1]
    for s in range(len(slabs) - 2, -1, -1):
        out = jnp.where(sub == s, slabs[s], out)
    return out


def _peer_router_kernel(x_ref, g_ref, sh_ref, sc_ref, wq_ref, keys_ref, ht_ref, rb_ref, eb_ref, ns_ref, ea_ref):
    a = _rms_mod(x_ref[...], g_ref[...], sh_ref[...], sc_ref[...])
    at = a.T.astype(BF16)
    ht_ref[...] = at
    qt = _mm(wq_ref[...], at).astype(BF16)
    k = PEER_TOPK
    n_slabs = PEER_N_KEYS // SUBLANES
    sub = lax.broadcasted_iota(jnp.int32, (SUBLANES, 1), 0)
    slab = lambda s, v: s[v * SUBLANES:(v + 1) * SUBLANES]
    for h in range(PEER_HEADS):
        sa = _mm(keys_ref[2 * h], qt[(2 * h) * LANES:(2 * h + 1) * LANES])
        sb = _mm(keys_ref[2 * h + 1], qt[(2 * h + 1) * LANES:(2 * h + 2) * LANES])
        top_a = _top16_of_slabs([slab(sa, v) for v in range(n_slabs)])
        top_b = _top16_of_slabs([slab(sb, v) for v in range(n_slabs)])
        a_lo, a_hi = _on_sublanes(top_a[:SUBLANES], sub), _on_sublanes(top_a[SUBLANES:], sub)
        b_hi = _on_sublanes(top_b[SUBLANES:], sub)
        cands = [a_lo + top_b[0], a_hi + top_b[0]]
        cands += [jnp.where(sub < k // (rj + 1), a_lo + top_b[rj], NEG_BIG) for rj in range(1, SUBLANES)]
        cands.append(top_a[0] + b_hi)
        best = _top16_of_slabs(cands)
        tau, m = best[k - 1], best[0]
        z = functools.reduce(jnp.add, [jnp.exp(c - m) for c in best])
        n_sel = [jnp.zeros_like(slab(sa, 0)) for _ in range(n_slabs)]
        for rj in range(k):
            if rj == 0:
                ok = jnp.minimum(jnp.where(cands[0] >= tau, a_lo, -NEG_BIG), jnp.where(cands[1] >= tau, a_hi, -NEG_BIG))
            elif rj < SUBLANES:
                ok = jnp.where(cands[rj + 1] >= tau, a_lo, -NEG_BIG)
            else:
                ok = jnp.where(top_a[0] + top_b[rj] >= tau, top_a[0], -NEG_BIG)
            if rj < SUBLANES:
                for shift in (4, 2, 1):
                    ok = jnp.minimum(ok, pltpu.roll(ok, shift, 0))
            n_sel = [n + jnp.where(slab(sa, v) >= ok, 1.0, 0.0) for v, n in enumerate(n_sel)]
        rank_b = [jnp.zeros_like(slab(sb, 0)) for _ in range(n_slabs)]
        for r in range(k):
            rank_b = [rk + jnp.where(top_b[r] > slab(sb, v), 1.0, 0.0) for v, rk in enumerate(rank_b)]
        inv_z = 1.0 / z
        for v in range(n_slabs):
            rows = slice(v * SUBLANES, (v + 1) * SUBLANES)
            rb_ref[h, rows, :] = rank_b[v]
            eb_ref[h, rows, :] = jnp.exp(slab(sb, v) - top_b[0]) * inv_z
            ns_ref[h, rows, :] = n_sel[v]
            ea_ref[h, rows, :] = jnp.exp(slab(sa, v) - top_a[0])


def peer_router(h2, g, sh, sc, wq_t, keys, tiles_per_batch, *, tt=512):
    t, d = h2.shape
    tab = lambda dtype: jax.ShapeDtypeStruct((PEER_HEADS, PEER_N_KEYS, t), dtype)
    tab_spec = pl.BlockSpec((PEER_HEADS, PEER_N_KEYS, tt), lambda i: (0, 0, i))
    row = pl.BlockSpec((None, 1, d), lambda i: (i // tiles_per_batch(tt), 0, 0))
    return pl.pallas_call(
        _peer_router_kernel,
        grid=(t // tt,),
        in_specs=[pl.BlockSpec((tt, d), lambda i: (i, 0)), pl.BlockSpec((1, d), lambda i: (0, 0)), row, row,
                  pl.BlockSpec(wq_t.shape, lambda i: (0, 0)), pl.BlockSpec(keys.shape, lambda i: (0, 0, 0))],
        out_specs=(pl.BlockSpec((d, tt), lambda i: (0, i)), tab_spec, tab_spec, tab_spec, tab_spec),
        out_shape=(jax.ShapeDtypeStruct((d, t), BF16), tab(F32), tab(F32), tab(F32), tab(F32)),
        compiler_params=_params("parallel"),
        name="peer_router",
    )(h2, g, sh, sc, wq_t, keys)


def _gelu_exact(x):
    return 0.5 * x * (1.0 + lax.erf(x * np.float32(np.sqrt(0.5))))


def _peer_expert_kernel(ht_ref, u_ref, vt_ref, rb_ref, eb_ref, ns_ref, ea_ref, h_ref, gate_ref, out_ref,
                        acc_ref, w_ref, *, rows_per_block):
    e = pl.program_id(1)
    tt = acc_ref.shape[1]
    ipc = 8
    n_chunks = rows_per_block // ipc
    rows_c = ipc * PEER_N_KEYS
    jt = 2 * SUBLANES

    def score_chunk(c):
        r0 = c * rows_c
        act = _gelu_exact(_mm(u_ref[r0:r0 + rows_c, :], ht_ref[...]))
        for lg in range(tt // LANES):
            lanes = slice(lg * LANES, (lg + 1) * LANES)
            for n in range(ipc):
                il = ipc * c + n
                rows = [(jnp.maximum(jnp.broadcast_to(ns_ref[h, il:il + 1, lanes], (SUBLANES, LANES)), 0.0),
                         jnp.maximum(jnp.broadcast_to(ea_ref[h, il:il + 1, lanes], (SUBLANES, LANES)), 0.0))
                        for h in range(PEER_HEADS)]
                for j0 in range(0, PEER_N_KEYS, jt):
                    halves = []
                    for k in range(jt // SUBLANES):
                        js = slice(j0 + k * SUBLANES, j0 + (k + 1) * SUBLANES)
                        g = None
                        for h, (n_i, ea) in enumerate(rows):
                            term = jnp.where(rb_ref[h, js, lanes] < n_i, eb_ref[h, js, lanes], 0.0) * ea
                            g = term if g is None else g + term
                        off = n * PEER_N_KEYS + j0 + k * SUBLANES
                        halves.append(g * act[off:off + SUBLANES, lanes])
                    off = n * PEER_N_KEYS + j0
                    w_ref[r0 + off:r0 + off + jt, lanes] = jnp.concatenate(halves, axis=0).astype(BF16)

    @pl.when(e == 0)
    def _():
        acc_ref[...] = jnp.zeros_like(acc_ref)

    for c in range(n_chunks):
        score_chunk(c)
    acc_ref[...] += _mm(vt_ref[...], w_ref[...])

    @pl.when(e == pl.num_programs(1) - 1)
    def _():
        out_ref[...] = h_ref[...] + gate_ref[...] * acc_ref[...].T


def peer_experts(ht, u_bf, vt_bf, layer, rb, eb, ns, ea, h2, gate, tiles_per_batch, *, tt=512, te=2048):
    t, d = h2.shape
    n_blocks = u_bf.shape[1] // te
    rows = te // PEER_N_KEYS
    full_tab = pl.BlockSpec((PEER_HEADS, PEER_N_KEYS, tt), lambda i, e: (0, 0, i))
    row_tab = pl.BlockSpec((PEER_HEADS, rows, tt), lambda i, e: (0, e, i))
    return pl.pallas_call(
        functools.partial(_peer_expert_kernel, rows_per_block=rows),
        grid=(t // tt, n_blocks),
        in_specs=[pl.BlockSpec((d, tt), lambda i, e: (0, i)),
                  pl.BlockSpec((None, te, d), lambda i, e: (layer, e, 0)),
                  pl.BlockSpec((None, d, te), lambda i, e: (layer, 0, e)),
                  full_tab, full_tab, row_tab, row_tab,
                  pl.BlockSpec((tt, d), lambda i, e: (i, 0)),
                  pl.BlockSpec((None, 1, d), lambda i, e: (i // tiles_per_batch(tt), 0, 0))],
        out_specs=pl.BlockSpec((tt, d), lambda i, e: (i, 0)),
        out_shape=jax.ShapeDtypeStruct((t, d), F32),
        scratch_shapes=[pltpu.VMEM((d, tt), F32), pltpu.VMEM((te, tt), BF16)],
        compiler_params=_params("parallel", "arbitrary"),
        name="peer_experts",
    )(ht, u_bf, vt_bf, rb, eb, ns, ea, h2, gate)


def peer_layer(h2, g, sh, sc, gate, w_query, sub_keys, u_bf, vt_bf, layer, tiles_per_batch):
    wq_t = w_query.T.astype(BF16)
    keys = sub_keys.reshape(PEER_HEADS * 2, PEER_N_KEYS, -1).astype(BF16)
    ht, rb, eb, ns, ea = peer_router(h2, g, sh, sc, wq_t, keys, tiles_per_batch)
    return peer_experts(ht, u_bf, vt_bf, layer, rb, eb, ns, ea, h2, gate, tiles_per_batch)


def _glu_kernel(x_ref, g_ref, sh_ref, sc_ref, w_ref, b_ref, out_ref):
    a = _rms_mod(x_ref[...], g_ref[...], sh_ref[...], sc_ref[...]).astype(BF16)
    y = _mm(a, w_ref[...]) + b_ref[...]
    d = out_ref.shape[-1]
    out_ref[...] = y[:, :d] * jax.nn.sigmoid(y[:, d:])


def conv_glu(h2, g, sh, sc, w_pw1, b_pw1, tiles_per_batch, *, tm=512):
    t, d = h2.shape
    row = pl.BlockSpec((None, 1, d), lambda i: (i // tiles_per_batch(tm), 0, 0))
    return pl.pallas_call(
        _glu_kernel,
        grid=(t // tm,),
        in_specs=[pl.BlockSpec((tm, d), lambda i: (i, 0)), pl.BlockSpec((1, d), lambda i: (0, 0)), row, row,
                  pl.BlockSpec((d, 2 * d), lambda i: (0, 0)), pl.BlockSpec((1, 2 * d), lambda i: (0, 0))],
        out_specs=pl.BlockSpec((tm, d), lambda i: (i, 0)),
        out_shape=jax.ShapeDtypeStruct((t, d), F32),
        compiler_params=_params("parallel"),
        name="conv_glu",
    )(h2, g, sh, sc, w_pw1, b_pw1)


def _conv_tail_kernel(prev_ref, cur_ref, next_ref, wdw_ref, bdw_ref, gln_ref, bln_ref, w2_ref, b2_ref,
                      h_ref, gate_ref, out_ref, win_ref, shift_ref, *, ts):
    s = pl.program_id(1)
    halo = CONV_HALO
    win_ref[0:halo, :] = jnp.where(s > 0, prev_ref[...], 0.0)
    win_ref[halo:halo + ts, :] = cur_ref[...]
    win_ref[halo + ts:, :] = jnp.where(s < pl.num_programs(1) - 1, next_ref[...], 0.0)
    acc = jnp.zeros((ts, cur_ref.shape[-1]), F32) + bdw_ref[...]
    base = halo - CONV_WIDTH // 2
    span = shift_ref.shape[0]
    for r in range(SUBLANES):
        src = win_ref
        if r:
            shift_ref[...] = win_ref[r:r + span, :]
            src = shift_ref
        for a in range(0, span - ts + 1, SUBLANES):
            k = a + r - base
            if 0 <= k < CONV_WIDTH:
                acc = acc + src[a:a + ts, :] * wdw_ref[k:k + 1, :]
    mu = jnp.mean(acc, axis=-1, keepdims=True)
    cen = acc - mu
    var = jnp.mean(cen * cen, axis=-1, keepdims=True)
    y = cen * lax.rsqrt(var + EPS) * gln_ref[...] + bln_ref[...]
    y = y * jax.nn.sigmoid(y)
    y = _mm(y.astype(BF16), w2_ref[...]) + b2_ref[...]
    out_ref[...] = h_ref[...] + gate_ref[...] * y


def conv_tail(a3, w_dw, b_dw, g_ln, b_ln, w_pw2, b_pw2, h3, gate, *, ts=256):
    b, s, d = a3.shape
    hb = ts // CONV_HALO
    n_halo = s // CONV_HALO
    vec = pl.BlockSpec((1, d), lambda i, j: (0, 0))
    return pl.pallas_call(
        functools.partial(_conv_tail_kernel, ts=ts),
        grid=(b, s // ts),
        in_specs=[pl.BlockSpec((None, CONV_HALO, d), lambda i, j: (i, jnp.maximum(j * hb - 1, 0), 0)),
                  pl.BlockSpec((None, ts, d), lambda i, j: (i, j, 0)),
                  pl.BlockSpec((None, CONV_HALO, d), lambda i, j: (i, jnp.minimum((j + 1) * hb, n_halo - 1), 0)),
                  pl.BlockSpec((CONV_WIDTH, d), lambda i, j: (0, 0)), vec, vec, vec,
                  pl.BlockSpec((d, d), lambda i, j: (0, 0)), vec,
                  pl.BlockSpec((None, ts, d), lambda i, j: (i, j, 0)),
                  pl.BlockSpec((None, 1, d), lambda i, j: (i, 0, 0))],
        out_specs=pl.BlockSpec((None, ts, d), lambda i, j: (i, j, 0)),
        out_shape=jax.ShapeDtypeStruct((b, s, d), F32),
        scratch_shapes=[pltpu.VMEM((ts + 2 * CONV_HALO, d), F32),
                        pltpu.VMEM((ts + 2 * CONV_HALO - SUBLANES, d), F32)],
        compiler_params=_params("parallel", "arbitrary"),
        name="conv_tail",
    )(a3, a3, a3, w_dw, b_dw, g_ln, b_ln, w_pw2, b_pw2, h3, gate)


def _pad_heads(w, width):
    k = w.shape[0]
    w = w.reshape(k, N_HEADS, width)
    return jnp.pad(w, ((0, 0), (0, 0), (0, LANES - width))).reshape(k, N_HEADS * LANES)


def _attn_weights(w_in, g_qa, w_q_up, g_kva, w_kv_up, g_q, g_k, na_g_q, na_g_k):
    a, b, c = 384, 384 + 256, 384 + 256 + MLA_ROPE
    kr_cols = jnp.pad(w_in[:, b:c], ((0, 0), (MLA_NOPE, LANES - MLA_QK)))
    w_in_r = jnp.concatenate([w_in[:, :a], w_in[:, a:b], kr_cols, w_in[:, c:]], axis=1).astype(BF16)
    wq = _pad_heads(w_q_up, MLA_QK).astype(BF16)
    kv = w_kv_up.reshape(-1, N_HEADS, MLA_NOPE + HEAD_DIM)
    wk = _pad_heads(kv[:, :, :MLA_NOPE].reshape(-1, N_HEADS * MLA_NOPE), MLA_NOPE).astype(BF16)
    wv = kv[:, :, MLA_NOPE:].reshape(-1, N_HEADS * HEAD_DIM).astype(BF16)
    pad_g = lambda g: jnp.pad(g, (0, LANES - MLA_QK)).reshape(1, LANES)
    gq = pad_g(g_q) * (MLA_QK ** -0.5)
    gk = pad_g(g_k)
    gnq = jnp.tile(na_g_q, 2).reshape(1, LANES) * (HEAD_DIM ** -0.5)
    gnk = jnp.tile(na_g_k, 2).reshape(1, LANES)
    return (w_in_r, g_qa.reshape(1, -1), wq, g_kva.reshape(1, -1), wk, wv, gq, gk, gnq, gnk)


def _rope_tables(s):
    t = jnp.arange(s)
    row = (t // GRID_W).astype(F32)
    col = (t % GRID_W).astype(F32)
    n_freq = MLA_ROPE // 4
    inv_freq = 1.0 / (ROPE_THETA ** (jnp.arange(n_freq, dtype=F32) / n_freq))
    ang = jnp.concatenate([row[:, None] * inv_freq, col[:, None] * inv_freq], axis=-1)
    cos, sin = jnp.cos(ang), jnp.sin(ang)
    half = MLA_ROPE // 2
    ones = jnp.ones((s, MLA_NOPE), F32)
    z = lambda n: jnp.zeros((s, n), F32)
    cos_t = jnp.concatenate([ones, cos, cos, z(LANES - MLA_QK)], axis=1)
    sin_lo = jnp.concatenate([z(MLA_NOPE), -sin, z(half), z(LANES - MLA_QK)], axis=1)
    sin_hi = jnp.concatenate([z(MLA_NOPE), z(half), sin, z(LANES - MLA_QK)], axis=1)
    return cos_t, sin_lo, sin_hi


def kernel(x, c, ctx, c_ctx, w_ada, b_ada, g_norm1, g_norm2, attn_w_in, mla_g_qa, mla_w_q_up, mla_g_kva, mla_w_kv_up, mla_g_q, mla_g_k, na_g_q, na_g_k, na_rpb, attn_w_out, conv_w_pw1, conv_b_pw1, conv_w_dw, conv_b_dw, conv_g_ln, conv_b_ln, conv_w_pw2, conv_b_pw2, peer_w_query, peer_sub_keys, peer_u, peer_v):
    b, s, d = x.shape
    n_ctx = ctx.shape[1]
    t = b * s
    depth = w_ada.shape[0]
    assert depth == 2
    tiles_per_batch = lambda tile: s // tile

    cond = jnp.zeros((16, d), F32).at[:b].set(c).at[b].set(c_ctx)
    mod = ada_params(cond, w_ada, b_ada).reshape(depth, 16, 6, 1, d)
    mods = [[mod[l, :, k] for k in range(6)] for l in range(depth)]

    u_bf = peer_u.astype(BF16)
    vt_bf = jnp.swapaxes(peer_v, 1, 2).astype(BF16)

    h2 = x.reshape(t, d)
    for l in range(depth):
        sh1, sc1, g1, sh2, sc2, g2 = mods[l]
        i = l // 2
        gn1 = g_norm1[l].reshape(1, d)
        if l % 2 == 0:
            wts = _attn_weights(attn_w_in[i], mla_g_qa[i], mla_w_q_up[i], mla_g_kva[i], mla_w_kv_up[i],
                                mla_g_q[i], mla_g_k[i], na_g_q[i], na_g_k[i])
            tabs = _rope_tables(s)
            tm = 256
            qm, km, vm, qn, kn, vn = attn_inputs(
                h2, lambda j: j // (s // tm), lambda j: j % (s // tm), tm, gn1, sh1, sc1, wts, tabs, latent=True)
            tc = n_ctx
            kmc, vmc, knc, vnc = attn_inputs(
                ctx.reshape(b * n_ctx, d), lambda j: b, lambda j: 0, tc, gn1, sh1, sc1, wts,
                tuple(tb[:tc] for tb in tabs), latent=False)
            r3 = lambda arr, n: arr.reshape(b, n, arr.shape[-1])
            om = mla_attention(r3(qm, s), r3(km, s), r3(vm, s), r3(kmc, n_ctx), r3(vmc, n_ctx))
            on = na_attention(r3(qn, s), r3(kn, s), r3(vn, s), r3(knc, n_ctx), r3(vnc, n_ctx),
                              na_bias_table(na_rpb[i]))
            w_out = attn_w_out[i].astype(BF16)
            half = w_out.shape[0] // 2
            h2 = out_proj(om.reshape(t, -1), on.reshape(t, -1), w_out[:half], w_out[half:], h2, g1,
                          tiles_per_batch)
        else:
            a2 = conv_glu(h2, gn1, sh1, sc1, conv_w_pw1[i].astype(BF16), conv_b_pw1[i].reshape(1, -1),
                          tiles_per_batch)
            h2 = conv_tail(a2.reshape(b, s, d), conv_w_dw[i], conv_b_dw[i].reshape(1, d),
                           conv_g_ln[i].reshape(1, d), conv_b_ln[i].reshape(1, d),
                           conv_w_pw2[i].astype(BF16), conv_b_pw2[i].reshape(1, d),
                           h2.reshape(b, s, d), g1).reshape(t, d)
        h2 = peer_layer(h2, g_norm2[l].reshape(1, d), sh2, sc2, g2, peer_w_query[l], peer_sub_keys[l],
                        u_bf, vt_bf, l, tiles_per_batch)
    return h2.reshape(b, s, d)
```

```python
import functools

import jax
import jax.numpy as jnp
import numpy as np
from jax import lax
from jax.experimental import pallas as pl
from jax.experimental.pallas import tpu as pltpu

F32 = jnp.float32
BF16 = jnp.bfloat16

LANES = 128
SUBLANES = 8
EPS = 1e-6
NEG_BIG = -1e30
VMEM_LIMIT = 56 * 1024 * 1024

GRID_W = 64
HEAD_DIM = 64
N_HEADS = 8
MLA_NOPE = 64
MLA_ROPE = 32
MLA_QK = MLA_NOPE + MLA_ROPE
NA_WIN_ROWS = 8
NA_WIN_COLS = 16
ROPE_THETA = 10000.0
CONV_WIDTH = 31
CONV_HALO = 16
PEER_HEADS = 8
PEER_N_KEYS = 128
PEER_TOPK = 16

NT_DIMS = (((1,), (1,)), ((), ()))


def _params(*sem):
    return pltpu.CompilerParams(dimension_semantics=sem, vmem_limit_bytes=VMEM_LIMIT)


def _mm(a, b):
    return jnp.dot(a, b, preferred_element_type=F32)


def _mm_nt(a, b):
    return lax.dot_general(a, b, NT_DIMS, preferred_element_type=F32)


def _rms_mod(x, g, shift, scale):
    inv = lax.rsqrt(jnp.mean(x * x, axis=-1, keepdims=True) + EPS)
    return (x * inv * g) * (1.0 + scale) + shift


def _ada_kernel(cond_ref, w_ref, b_ref, out_ref):
    cond = cond_ref[...]
    a = cond * jax.nn.sigmoid(cond)
    out_ref[...] = jnp.dot(a, w_ref[...], preferred_element_type=F32,
                           precision=lax.Precision.HIGHEST) + b_ref[...]


def ada_params(cond, w_ada, b_ada, *, tn=1536):
    depth, d, n = w_ada.shape
    rows = cond.shape[0]
    return pl.pallas_call(
        _ada_kernel,
        grid=(depth, n // tn),
        in_specs=[pl.BlockSpec((rows, d), lambda l, j: (0, 0)),
                  pl.BlockSpec((None, d, tn), lambda l, j: (l, 0, j)),
                  pl.BlockSpec((None, 1, tn), lambda l, j: (l, 0, j))],
        out_specs=pl.BlockSpec((None, rows, tn), lambda l, j: (l, 0, j)),
        out_shape=jax.ShapeDtypeStruct((depth, rows, n), F32),
        compiler_params=_params("parallel", "parallel"),
        name="ada_params",
    )(cond, w_ada, b_ada.reshape(depth, 1, n))


def _rope128(x, cos, sin_lo, sin_hi):
    return x * cos + pltpu.roll(x, LANES - MLA_ROPE // 2, 1) * sin_lo + pltpu.roll(x, MLA_ROPE // 2, 1) * sin_hi


def _head_rms128(x, n_valid):
    ss = jnp.sum(x * x, axis=-1, keepdims=True) * (1.0 / n_valid)
    return x * lax.rsqrt(ss + EPS)


def _pair_rms64(x, lane_lo):
    xx = x * x
    ss_lo = jnp.sum(jnp.where(lane_lo, xx, 0.0), axis=-1, keepdims=True)
    ss_hi = jnp.sum(jnp.where(lane_lo, 0.0, xx), axis=-1, keepdims=True)
    inv = jnp.where(lane_lo, lax.rsqrt(ss_lo * (1.0 / HEAD_DIM) + EPS), lax.rsqrt(ss_hi * (1.0 / HEAD_DIM) + EPS))
    return x * inv


def _attn_in_kernel(x_ref, g1_ref, sh_ref, sc_ref, w_in_ref, gqa_ref, wq_ref, gkva_ref, wk_ref, wv_ref,
                    gq_ref, gk_ref, gnq_ref, gnk_ref, cos_ref, slo_ref, shi_ref, *out_refs, latent):
    if latent:
        qm_ref, km_ref, vm_ref, qn_ref, kn_ref, vn_ref = out_refs
    else:
        km_ref, vm_ref, kn_ref, vn_ref = out_refs
    a = _rms_mod(x_ref[...], g1_ref[...], sh_ref[...], sc_ref[...]).astype(BF16)
    z = _mm(a, w_in_ref[...])
    c0, c1, c2 = 384, 384 + 256, 384 + 256 + LANES
    cq, ckv, kr = z[:, :c0], z[:, c0:c1], z[:, c1:c2]
    naq, nak, nav = z[:, c2:c2 + 512], z[:, c2 + 512:c2 + 1024], z[:, c2 + 1024:c2 + 1536]

    def rms(v, g):
        return v * lax.rsqrt(jnp.mean(v * v, axis=-1, keepdims=True) + EPS) * g

    ckv_n = rms(ckv, gkva_ref[...]).astype(BF16)
    kfull = _mm(ckv_n, wk_ref[...])
    vm_ref[...] = _mm(ckv_n, wv_ref[...]).astype(BF16)
    if latent:
        qfull = _mm(rms(cq, gqa_ref[...]).astype(BF16), wq_ref[...])
        cos, slo, shi = cos_ref[...], slo_ref[...], shi_ref[...]
    gq, gk = gq_ref[...], gk_ref[...]
    for h in range(N_HEADS):
        sl = slice(h * LANES, (h + 1) * LANES)
        kh = _head_rms128(kfull[:, sl] + kr, MLA_QK) * gk
        if latent:
            kh = _rope128(kh, cos, slo, shi)
            qh = _rope128(_head_rms128(qfull[:, sl], MLA_QK) * gq, cos, slo, shi)
            qm_ref[:, sl] = qh.astype(BF16)
        km_ref[:, sl] = kh.astype(BF16)
    lane_lo = lax.broadcasted_iota(jnp.int32, (1, LANES), 1) < HEAD_DIM
    gnq, gnk = gnq_ref[...], gnk_ref[...]
    for p in range(N_HEADS // 2):
        sl = slice(p * LANES, (p + 1) * LANES)
        kn_ref[:, sl] = (_pair_rms64(nak[:, sl], lane_lo) * gnk).astype(BF16)
        if latent:
            qn_ref[:, sl] = (_pair_rms64(naq[:, sl], lane_lo) * gnq).astype(BF16)
    vn_ref[...] = nav.astype(BF16)


def attn_inputs(x2, batch_of_tile, pos_of_tile, tm, g1, sh, sc, wts, rope_tabs, *, latent):
    t, d = x2.shape
    w_in, gqa, wq, gkva, wk, wv, gq, gk, gnq, gnk = wts
    full = lambda arr: pl.BlockSpec(arr.shape, lambda i: (0,) * arr.ndim)
    row = pl.BlockSpec((None, 1, d), lambda i: (batch_of_tile(i), 0, 0))
    tab = pl.BlockSpec((tm, LANES), lambda i: (pos_of_tile(i), 0))
    wide = jax.ShapeDtypeStruct((t, N_HEADS * LANES), BF16)
    half = jax.ShapeDtypeStruct((t, N_HEADS * HEAD_DIM), BF16)
    o_wide = pl.BlockSpec((tm, N_HEADS * LANES), lambda i: (i, 0))
    o_half = pl.BlockSpec((tm, N_HEADS * HEAD_DIM), lambda i: (i, 0))
    if latent:
        out_shape, out_specs = (wide, wide, half, half, half, half), (o_wide, o_wide, o_half, o_half, o_half, o_half)
    else:
        out_shape, out_specs = (wide, half, half, half), (o_wide, o_half, o_half, o_half)
    return pl.pallas_call(
        functools.partial(_attn_in_kernel, latent=latent),
        grid=(t // tm,),
        in_specs=[pl.BlockSpec((tm, d), lambda i: (i, 0)), full(g1), row, row,
                  full(w_in), full(gqa), full(wq), full(gkva), full(wk), full(wv),
                  full(gq), full(gk), full(gnq), full(gnk), tab, tab, tab],
        out_specs=out_specs,
        out_shape=out_shape,
        compiler_params=_params("parallel"),
        name="attn_inputs_lat" if latent else "attn_inputs_ctx",
    )(x2, g1, sh, sc, w_in, gqa, wq, gkva, wk, wv, gq, gk, gnq, gnk, *rope_tabs)


def _softmax_pv(s_list, v_list):
    m = functools.reduce(jnp.maximum, [jnp.max(s, axis=-1, keepdims=True) for s in s_list])
    ps = [jnp.exp(s - m) for s in s_list]
    denom = functools.reduce(jnp.add, [jnp.sum(p, axis=-1, keepdims=True) for p in ps])
    o = functools.reduce(jnp.add, [_mm(p.astype(BF16), v) for p, v in zip(ps, v_list)])
    return o / denom


def _mla_kernel(q_ref, k_ref, v_ref, kc_ref, vc_ref, o_ref):
    lane_lo = lax.broadcasted_iota(jnp.int32, (1, LANES), 1) < HEAD_DIM
    for p in range(N_HEADS // 2):
        vsl = slice(p * LANES, (p + 1) * LANES)
        outs = []
        for h in (2 * p, 2 * p + 1):
            sl = slice(h * LANES, (h + 1) * LANES)
            q = q_ref[:, sl]
            outs.append(_softmax_pv([_mm_nt(q, k_ref[:, sl]), _mm_nt(q, kc_ref[:, sl])],
                                    [v_ref[:, vsl], vc_ref[:, vsl]]))
        o_ref[:, vsl] = jnp.where(lane_lo, outs[0], outs[1]).astype(BF16)


def mla_attention(qm, km, vm, kmc, vmc, *, tq=512):
    b, s, dq = qm.shape
    c, dv = kmc.shape[1], vm.shape[2]
    return pl.pallas_call(
        _mla_kernel,
        grid=(b, s // tq),
        in_specs=[pl.BlockSpec((None, tq, dq), lambda i, j: (i, j, 0)),
                  pl.BlockSpec((None, s, dq), lambda i, j: (i, 0, 0)),
                  pl.BlockSpec((None, s, dv), lambda i, j: (i, 0, 0)),
                  pl.BlockSpec((None, c, dq), lambda i, j: (i, 0, 0)),
                  pl.BlockSpec((None, c, dv), lambda i, j: (i, 0, 0))],
        out_specs=pl.BlockSpec((None, tq, dv), lambda i, j: (i, j, 0)),
        out_shape=jax.ShapeDtypeStruct((b, s, dv), BF16),
        compiler_params=_params("parallel", "arbitrary"),
        name="mla_attention",
    )(qm, km, vm, kmc, vmc)


def _na_row0(r, n_rows):
    return jnp.clip(r - NA_WIN_ROWS // 2, 0, n_rows - NA_WIN_ROWS)


def _na_kernel(q_ref, k_ref, v_ref, kc_ref, vc_ref, bias_ref, o_ref, *, n_rows, rows_per_step):
    band = NA_WIN_ROWS * GRID_W
    lane_lo = lax.broadcasted_iota(jnp.int32, (1, LANES), 1) < HEAD_DIM
    for rr in range(rows_per_step):
        r = pl.program_id(1) * rows_per_step + rr
        row0 = _na_row0(r, n_rows)
        start = pl.multiple_of(row0 * GRID_W, GRID_W)
        cls = r - row0
        qrows = slice(rr * GRID_W, (rr + 1) * GRID_W)
        for p in range(N_HEADS // 2):
            sl = slice(p * LANES, (p + 1) * LANES)
            qp = q_ref[qrows, sl]
            kp, vp = k_ref[pl.ds(start, band), sl], v_ref[pl.ds(start, band), sl]
            kcp, vcp = kc_ref[:, sl], vc_ref[:, sl]
            zero = jnp.zeros_like(qp)
            q2 = jnp.concatenate([jnp.where(lane_lo, qp, zero), jnp.where(lane_lo, zero, qp)], axis=0)
            bias2 = jnp.concatenate([bias_ref[cls, 2 * p], bias_ref[cls, 2 * p + 1]], axis=0)
            o2 = _softmax_pv([_mm_nt(q2, kp) + bias2, _mm_nt(q2, kcp)], [vp, vcp])
            o_ref[qrows, sl] = jnp.where(lane_lo, o2[:GRID_W], o2[GRID_W:]).astype(BF16)


def na_attention(qn, kn, vn, knc, vnc, bias, *, rows_per_step=4):
    b, s, dh = qn.shape
    c = knc.shape[1]
    n_rows = s // GRID_W
    tq = rows_per_step * GRID_W
    whole = lambda n: pl.BlockSpec((None, n, dh), lambda i, r: (i, 0, 0))
    return pl.pallas_call(
        functools.partial(_na_kernel, n_rows=n_rows, rows_per_step=rows_per_step),
        grid=(b, n_rows // rows_per_step),
        in_specs=[pl.BlockSpec((None, tq, dh), lambda i, r: (i, r, 0)),
                  whole(s), whole(s), whole(c), whole(c),
                  pl.BlockSpec(bias.shape, lambda i, r: (0, 0, 0, 0))],
        out_specs=pl.BlockSpec((None, tq, dh), lambda i, r: (i, r, 0)),
        out_shape=jax.ShapeDtypeStruct((b, s, dh), BF16),
        compiler_params=_params("parallel", "arbitrary"),
        name="na_attention",
    )(qn, kn, vn, knc, vnc, bias)


def na_bias_table(rpb):
    n_heads, n_rel_r, n_rel_c = rpb.shape
    q = np.arange(GRID_W)
    c0 = np.clip(q - NA_WIN_COLS // 2, 0, GRID_W - NA_WIN_COLS)
    kc = np.arange(GRID_W)
    inside = (kc[None, :] >= c0[:, None]) & (kc[None, :] < c0[:, None] + NA_WIN_COLS)
    rel_c = kc[None, :] - q[:, None] + NA_WIN_COLS - 1
    onehot = ((rel_c[None] == np.arange(n_rel_c)[:, None, None]) & inside[None]).astype(np.float32)
    toep = jnp.einsum("hrc,cqk->hrqk", rpb, jnp.asarray(onehot), precision=lax.Precision.HIGHEST)
    toep = jnp.where(inside[None, None], toep, NEG_BIG)
    per_cls = [toep[:, NA_WIN_ROWS - 1 - cls:2 * NA_WIN_ROWS - 1 - cls] for cls in range(NA_WIN_ROWS)]
    tab = jnp.stack(per_cls)
    return tab.transpose(0, 1, 3, 2, 4).reshape(NA_WIN_ROWS, n_heads, GRID_W, NA_WIN_ROWS * GRID_W)


def _out_proj_kernel(om_ref, on_ref, wa_ref, wb_ref, h_ref, gate_ref, out_ref):
    y = _mm(om_ref[...], wa_ref[...]) + _mm(on_ref[...], wb_ref[...])
    out_ref[...] = h_ref[...] + gate_ref[...] * y


def out_proj(om, on, wa, wb, h2, gate, tiles_per_batch, *, tm=512):
    t, d = h2.shape
    k = om.shape[1]
    return pl.pallas_call(
        _out_proj_kernel,
        grid=(t // tm,),
        in_specs=[pl.BlockSpec((tm, k), lambda i: (i, 0)), pl.BlockSpec((tm, k), lambda i: (i, 0)),
                  pl.BlockSpec((k, d), lambda i: (0, 0)), pl.BlockSpec((k, d), lambda i: (0, 0)),
                  pl.BlockSpec((tm, d), lambda i: (i, 0)),
                  pl.BlockSpec((None, 1, d), lambda i: (i // tiles_per_batch(tm), 0, 0))],
        out_specs=pl.BlockSpec((tm, d), lambda i: (i, 0)),
        out_shape=jax.ShapeDtypeStruct((t, d), F32),
        compiler_params=_params("parallel"),
        name="out_proj",
    )(om, on, wa, wb, h2, gate)


def _sort16_comparators():
    comps = []

    def merge(lo, n, r):
        step = r * 2
        if step < n:
            merge(lo, n, step)
            merge(lo + r, n, step)
            comps.extend((i, i + r) for i in range(lo + r, lo + n - r, step))
        else:
            comps.append((lo, lo + r))

    def sort(lo, n):
        if n > 1:
            sort(lo, n // 2)
            sort(lo + n // 2, n // 2)
            merge(lo, n, 1)

    sort(0, PEER_TOPK)
    return comps


def _cmp_exchange(x, i, j):
    a, b = x[i], x[j]
    if a is None:
        x[i], x[j] = b, None
    elif b is not None:
        x[i], x[j] = jnp.maximum(a, b), jnp.minimum(a, b)


def _top16_of_slabs(slabs):
    x = list(slabs) + [None] * (PEER_TOPK - len(slabs))
    for i, j in _sort16_comparators():
        _cmp_exchange(x, i, j)
    for shift in (4, 2, 1):
        y = [None if v is None else pltpu.roll(v, shift, 0) for v in x]
        z = []
        for v in range(PEER_TOPK):
            a, b = x[v], y[PEER_TOPK - 1 - v]
            z.append(b if a is None else a if b is None else jnp.maximum(a, b))
        d = PEER_TOPK // 2
        while d:
            for i in range(PEER_TOPK):
                if i & d == 0:
                    _cmp_exchange(z, i, i + d)
            d //= 2
        x = z
    return x


def _on_sublanes(slabs, sub):
    out = slabs[-1]
    for s in range(len(slabs) - 2, -1, -1):
        out = jnp.where(sub == s, slabs[s], out)
    return out


def _peer_router_kernel(x_ref, g_ref, sh_ref, sc_ref, wq_ref, keys_ref, ht_ref, sb_ref, eb_ref, th_ref, ea_ref):
    a = _rms_mod(x_ref[...], g_ref[...], sh_ref[...], sc_ref[...])
    at = a.T.astype(BF16)
    ht_ref[...] = at
    qt = _mm(wq_ref[...], at).astype(BF16)
    k = PEER_TOPK
    n_slabs = PEER_N_KEYS // SUBLANES
    sub = lax.broadcasted_iota(jnp.int32, (SUBLANES, 1), 0)
    slab = lambda s, v: s[v * SUBLANES:(v + 1) * SUBLANES]
    for h in range(PEER_HEADS):
        sa = _mm(keys_ref[2 * h], qt[(2 * h) * LANES:(2 * h + 1) * LANES])
        sb = _mm(keys_ref[2 * h + 1], qt[(2 * h + 1) * LANES:(2 * h + 2) * LANES])
        top_a = _top16_of_slabs([slab(sa, v) for v in range(n_slabs)])
        top_b = _top16_of_slabs([slab(sb, v) for v in range(n_slabs)])
        a_lo, a_hi = _on_sublanes(top_a[:SUBLANES], sub), _on_sublanes(top_a[SUBLANES:], sub)
        b_hi = _on_sublanes(top_b[SUBLANES:], sub)
        cands = [a_lo + top_b[0], a_hi + top_b[0]]
        cands += [jnp.where(sub < k // (rj + 1), a_lo + top_b[rj], NEG_BIG) for rj in range(1, SUBLANES)]
        cands.append(top_a[0] + b_hi)
        best = _top16_of_slabs(cands)
        tau, m = best[k - 1], best[0]
        z = functools.reduce(jnp.add, [jnp.exp(c - m) for c in best])
        theta = [jnp.full_like(slab(sa, 0), -NEG_BIG) for _ in range(n_slabs)]
        n_cut = 4
        for rj in range(n_cut):
            if rj == 0:
                ok = jnp.minimum(jnp.where(cands[0] >= tau, a_lo, -NEG_BIG), jnp.where(cands[1] >= tau, a_hi, -NEG_BIG))
            else:
                ok = jnp.where(cands[rj + 1] >= tau, a_lo, -NEG_BIG)
            for shift in (4, 2, 1):
                ok = jnp.minimum(ok, pltpu.roll(ok, shift, 0))
            theta = [jnp.minimum(th, jnp.where(slab(sa, v) >= ok, top_b[rj], -NEG_BIG)) for v, th in enumerate(theta)]
        for ri in range(k // (n_cut + 1)):
            low = [jnp.where(top_a[ri] + top_b[rj] >= tau, top_b[rj], -NEG_BIG)
                   for rj in range(n_cut, k) if (ri + 1) * (rj + 1) <= k]
            low = functools.reduce(jnp.minimum, low)
            theta = [jnp.minimum(th, jnp.where(slab(sa, v) == top_a[ri], low, -NEG_BIG)) for v, th in enumerate(theta)]
        inv_z = 1.0 / z
        for v in range(n_slabs):
            rows = slice(v * SUBLANES, (v + 1) * SUBLANES)
            sb_ref[h, rows, :] = slab(sb, v)
            eb_ref[h, rows, :] = jnp.exp(slab(sb, v) - top_b[0]) * inv_z
            th_ref[h, rows, :] = theta[v]
            ea_ref[h, rows, :] = jnp.exp(slab(sa, v) - top_a[0])


def peer_router(h2, g, sh, sc, wq_t, keys, tiles_per_batch, *, tt=512):
    t, d = h2.shape
    tab = lambda dtype: jax.ShapeDtypeStruct((PEER_HEADS, PEER_N_KEYS, t), dtype)
    tab_spec = pl.BlockSpec((PEER_HEADS, PEER_N_KEYS, tt), lambda i: (0, 0, i))
    row = pl.BlockSpec((None, 1, d), lambda i: (i // tiles_per_batch(tt), 0, 0))
    return pl.pallas_call(
        _peer_router_kernel,
        grid=(t // tt,),
        in_specs=[pl.BlockSpec((tt, d), lambda i: (i, 0)), pl.BlockSpec((1, d), lambda i: (0, 0)), row, row,
                  pl.BlockSpec(wq_t.shape, lambda i: (0, 0)), pl.BlockSpec(keys.shape, lambda i: (0, 0, 0))],
        out_specs=(pl.BlockSpec((d, tt), lambda i: (0, i)), tab_spec, tab_spec, tab_spec, tab_spec),
        out_shape=(jax.ShapeDtypeStruct((d, t), BF16), tab(F32), tab(F32), tab(F32), tab(F32)),
        compiler_params=_params("parallel"),
        name="peer_router",
    )(h2, g, sh, sc, wq_t, keys)


def _gelu_exact(x):
    return 0.5 * x * (1.0 + lax.erf(x * np.float32(np.sqrt(0.5))))


def _peer_expert_kernel(ht_ref, u_ref, vt_ref, sb_ref, eb_ref, th_ref, ea_ref, h_ref, gate_ref, out_ref,
                        acc_ref, w_ref, *, rows_per_block):
    e = pl.program_id(1)
    tt = acc_ref.shape[1]
    ipc = 8
    n_chunks = rows_per_block // ipc
    rows_c = ipc * PEER_N_KEYS
    jt = 2 * SUBLANES

    def score_chunk(c):
        r0 = c * rows_c
        act = _gelu_exact(_mm(u_ref[r0:r0 + rows_c, :], ht_ref[...]))
        for lg in range(tt // LANES):
            lanes = slice(lg * LANES, (lg + 1) * LANES)
            for n in range(ipc):
                il = ipc * c + n
                rows = [(jnp.maximum(jnp.broadcast_to(th_ref[h, il:il + 1, lanes], (SUBLANES, LANES)), NEG_BIG),
                         jnp.maximum(jnp.broadcast_to(ea_ref[h, il:il + 1, lanes], (SUBLANES, LANES)), 0.0))
                        for h in range(PEER_HEADS)]
                for j0 in range(0, PEER_N_KEYS, jt):
                    halves = []
                    for k in range(jt // SUBLANES):
                        js = slice(j0 + k * SUBLANES, j0 + (k + 1) * SUBLANES)
                        g = None
                        for h, (theta, ea) in enumerate(rows):
                            term = jnp.where(sb_ref[h, js, lanes] >= theta, eb_ref[h, js, lanes], 0.0) * ea
                            g = term if g is None else g + term
                        off = n * PEER_N_KEYS + j0 + k * SUBLANES
                        halves.append(g * act[off:off + SUBLANES, lanes])
                    off = n * PEER_N_KEYS + j0
                    w_ref[r0 + off:r0 + off + jt, lanes] = jnp.concatenate(halves, axis=0).astype(BF16)

    @pl.when(e == 0)
    def _():
        acc_ref[...] = jnp.zeros_like(acc_ref)

    for c in range(n_chunks):
        score_chunk(c)
    acc_ref[...] += _mm(vt_ref[...], w_ref[...])

    @pl.when(e == pl.num_programs(1) - 1)
    def _():
        out_ref[...] = h_ref[...] + gate_ref[...] * acc_ref[...].T


def peer_experts(ht, u_bf, vt_bf, layer, sb, eb, th, ea, h2, gate, tiles_per_batch, *, tt=512, te=2048):
    t, d = h2.shape
    n_blocks = u_bf.shape[1] // te
    rows = te // PEER_N_KEYS
    full_tab = pl.BlockSpec((PEER_HEADS, PEER_N_KEYS, tt), lambda i, e: (0, 0, i))
    row_tab = pl.BlockSpec((PEER_HEADS, rows, tt), lambda i, e: (0, e, i))
    return pl.pallas_call(
        functools.partial(_peer_expert_kernel, rows_per_block=rows),
        grid=(t // tt, n_blocks),
        in_specs=[pl.BlockSpec((d, tt), lambda i, e: (0, i)),
                  pl.BlockSpec((None, te, d), lambda i, e: (layer, e, 0)),
                  pl.BlockSpec((None, d, te), lambda i, e: (layer, 0, e)),
                  full_tab, full_tab, row_tab, row_tab,
                  pl.BlockSpec((tt, d), lambda i, e: (i, 0)),
                  pl.BlockSpec((None, 1, d), lambda i, e: (i // tiles_per_batch(tt), 0, 0))],
        out_specs=pl.BlockSpec((tt, d), lambda i, e: (i, 0)),
        out_shape=jax.ShapeDtypeStruct((t, d), F32),
        scratch_shapes=[pltpu.VMEM((d, tt), F32), pltpu.VMEM((te, tt), BF16)],
        compiler_params=_params("parallel", "arbitrary"),
        name="peer_experts",
    )(ht, u_bf, vt_bf, sb, eb, th, ea, h2, gate)


def peer_layer(h2, g, sh, sc, gate, w_query, sub_keys, u_bf, vt_bf, layer, tiles_per_batch):
    wq_t = w_query.T.astype(BF16)
    keys = sub_keys.reshape(PEER_HEADS * 2, PEER_N_KEYS, -1).astype(BF16)
    ht, sb, eb, th, ea = peer_router(h2, g, sh, sc, wq_t, keys, tiles_per_batch)
    return peer_experts(ht, u_bf, vt_bf, layer, sb, eb, th, ea, h2, gate, tiles_per_batch)


def _glu_kernel(x_ref, g_ref, sh_ref, sc_ref, w_ref, b_ref, out_ref):
    a = _rms_mod(x_ref[...], g_ref[...], sh_ref[...], sc_ref[...]).astype(BF16)
    y = _mm(a, w_ref[...]) + b_ref[...]
    d = out_ref.shape[-1]
    out_ref[...] = y[:, :d] * jax.nn.sigmoid(y[:, d:])


def conv_glu(h2, g, sh, sc, w_pw1, b_pw1, tiles_per_batch, *, tm=512):
    t, d = h2.shape
    row = pl.BlockSpec((None, 1, d), lambda i: (i // tiles_per_batch(tm), 0, 0))
    return pl.pallas_call(
        _glu_kernel,
        grid=(t // tm,),
        in_specs=[pl.BlockSpec((tm, d), lambda i: (i, 0)), pl.BlockSpec((1, d), lambda i: (0, 0)), row, row,
                  pl.BlockSpec((d, 2 * d), lambda i: (0, 0)), pl.BlockSpec((1, 2 * d), lambda i: (0, 0))],
        out_specs=pl.BlockSpec((tm, d), lambda i: (i, 0)),
        out_shape=jax.ShapeDtypeStruct((t, d), F32),
        compiler_params=_params("parallel"),
        name="conv_glu",
    )(h2, g, sh, sc, w_pw1, b_pw1)


def _conv_tail_kernel(prev_ref, cur_ref, next_ref, wdw_ref, bdw_ref, gln_ref, bln_ref, w2_ref, b2_ref,
                      h_ref, gate_ref, out_ref, win_ref, shift_ref, *, ts):
    s = pl.program_id(1)
    halo = CONV_HALO
    win_ref[0:halo, :] = jnp.where(s > 0, prev_ref[...], 0.0)
    win_ref[halo:halo + ts, :] = cur_ref[...]
    win_ref[halo + ts:, :] = jnp.where(s < pl.num_programs(1) - 1, next_ref[...], 0.0)
    acc = jnp.zeros((ts, cur_ref.shape[-1]), F32) + bdw_ref[...]
    base = halo - CONV_WIDTH // 2
    span = shift_ref.shape[0]
    for r in range(SUBLANES):
        src = win_ref
        if r:
            shift_ref[...] = win_ref[r:r + span, :]
            src = shift_ref
        for a in range(0, span - ts + 1, SUBLANES):
            k = a + r - base
            if 0 <= k < CONV_WIDTH:
                acc = acc + src[a:a + ts, :] * wdw_ref[k:k + 1, :]
    mu = jnp.mean(acc, axis=-1, keepdims=True)
    cen = acc - mu
    var = jnp.mean(cen * cen, axis=-1, keepdims=True)
    y = cen * lax.rsqrt(var + EPS) * gln_ref[...] + bln_ref[...]
    y = y * jax.nn.sigmoid(y)
    y = _mm(y.astype(BF16), w2_ref[...]) + b2_ref[...]
    out_ref[...] = h_ref[...] + gate_ref[...] * y


def conv_tail(a3, w_dw, b_dw, g_ln, b_ln, w_pw2, b_pw2, h3, gate, *, ts=256):
    b, s, d = a3.shape
    hb = ts // CONV_HALO
    n_halo = s // CONV_HALO
    vec = pl.BlockSpec((1, d), lambda i, j: (0, 0))
    return pl.pallas_call(
        functools.partial(_conv_tail_kernel, ts=ts),
        grid=(b, s // ts),
        in_specs=[pl.BlockSpec((None, CONV_HALO, d), lambda i, j: (i, jnp.maximum(j * hb - 1, 0), 0)),
                  pl.BlockSpec((None, ts, d), lambda i, j: (i, j, 0)),
                  pl.BlockSpec((None, CONV_HALO, d), lambda i, j: (i, jnp.minimum((j + 1) * hb, n_halo - 1), 0)),
                  pl.BlockSpec((CONV_WIDTH, d), lambda i, j: (0, 0)), vec, vec, vec,
                  pl.BlockSpec((d, d), lambda i, j: (0, 0)), vec,
                  pl.BlockSpec((None, ts, d), lambda i, j: (i, j, 0)),
                  pl.BlockSpec((None, 1, d), lambda i, j: (i, 0, 0))],
        out_specs=pl.BlockSpec((None, ts, d), lambda i, j: (i, j, 0)),
        out_shape=jax.ShapeDtypeStruct((b, s, d), F32),
        scratch_shapes=[pltpu.VMEM((ts + 2 * CONV_HALO, d), F32),
                        pltpu.VMEM((ts + 2 * CONV_HALO - SUBLANES, d), F32)],
        compiler_params=_params("parallel", "arbitrary"),
        name="conv_tail",
    )(a3, a3, a3, w_dw, b_dw, g_ln, b_ln, w_pw2, b_pw2, h3, gate)


def _pad_heads(w, width):
    k = w.shape[0]
    w = w.reshape(k, N_HEADS, width)
    return jnp.pad(w, ((0, 0), (0, 0), (0, LANES - width))).reshape(k, N_HEADS * LANES)


def _attn_weights(w_in, g_qa, w_q_up, g_kva, w_kv_up, g_q, g_k, na_g_q, na_g_k):
    a, b, c = 384, 384 + 256, 384 + 256 + MLA_ROPE
    kr_cols = jnp.pad(w_in[:, b:c], ((0, 0), (MLA_NOPE, LANES - MLA_QK)))
    w_in_r = jnp.concatenate([w_in[:, :a], w_in[:, a:b], kr_cols, w_in[:, c:]], axis=1).astype(BF16)
    wq = _pad_heads(w_q_up, MLA_QK).astype(BF16)
    kv = w_kv_up.reshape(-1, N_HEADS, MLA_NOPE + HEAD_DIM)
    wk = _pad_heads(kv[:, :, :MLA_NOPE].reshape(-1, N_HEADS * MLA_NOPE), MLA_NOPE).astype(BF16)
    wv = kv[:, :, MLA_NOPE:].reshape(-1, N_HEADS * HEAD_DIM).astype(BF16)
    pad_g = lambda g: jnp.pad(g, (0, LANES - MLA_QK)).reshape(1, LANES)
    gq = pad_g(g_q) * (MLA_QK ** -0.5)
    gk = pad_g(g_k)
    gnq = jnp.tile(na_g_q, 2).reshape(1, LANES) * (HEAD_DIM ** -0.5)
    gnk = jnp.tile(na_g_k, 2).reshape(1, LANES)
    return (w_in_r, g_qa.reshape(1, -1), wq, g_kva.reshape(1, -1), wk, wv, gq, gk, gnq, gnk)


def _rope_tables(s):
    t = jnp.arange(s)
    row = (t // GRID_W).astype(F32)
    col = (t % GRID_W).astype(F32)
    n_freq = MLA_ROPE // 4
    inv_freq = 1.0 / (ROPE_THETA ** (jnp.arange(n_freq, dtype=F32) / n_freq))
    ang = jnp.concatenate([row[:, None] * inv_freq, col[:, None] * inv_freq], axis=-1)
    cos, sin = jnp.cos(ang), jnp.sin(ang)
    half = MLA_ROPE // 2
    ones = jnp.ones((s, MLA_NOPE), F32)
    z = lambda n: jnp.zeros((s, n), F32)
    cos_t = jnp.concatenate([ones, cos, cos, z(LANES - MLA_QK)], axis=1)
    sin_lo = jnp.concatenate([z(MLA_NOPE), -sin, z(half), z(LANES - MLA_QK)], axis=1)
    sin_hi = jnp.concatenate([z(MLA_NOPE), z(half), sin, z(LANES - MLA_QK)], axis=1)
    return cos_t, sin_lo, sin_hi


def kernel(x, c, ctx, c_ctx, w_ada, b_ada, g_norm1, g_norm2, attn_w_in, mla_g_qa, mla_w_q_up, mla_g_kva, mla_w_kv_up, mla_g_q, mla_g_k, na_g_q, na_g_k, na_rpb, attn_w_out, conv_w_pw1, conv_b_pw1, conv_w_dw, conv_b_dw, conv_g_ln, conv_b_ln, conv_w_pw2, conv_b_pw2, peer_w_query, peer_sub_keys, peer_u, peer_v):
    b, s, d = x.shape
    n_ctx = ctx.shape[1]
    t = b * s
    depth = w_ada.shape[0]
    assert depth == 2
    tiles_per_batch = lambda tile: s // tile

    cond = jnp.zeros((16, d), F32).at[:b].set(c).at[b].set(c_ctx)
    mod = ada_params(cond, w_ada, b_ada).reshape(depth, 16, 6, 1, d)
    mods = [[mod[l, :, k] for k in range(6)] for l in range(depth)]

    u_bf = peer_u.astype(BF16)
    vt_bf = jnp.swapaxes(peer_v, 1, 2).astype(BF16)

    h2 = x.reshape(t, d)
    for l in range(depth):
        sh1, sc1, g1, sh2, sc2, g2 = mods[l]
        i = l // 2
        gn1 = g_norm1[l].reshape(1, d)
        if l % 2 == 0:
            wts = _attn_weights(attn_w_in[i], mla_g_qa[i], mla_w_q_up[i], mla_g_kva[i], mla_w_kv_up[i],
                                mla_g_q[i], mla_g_k[i], na_g_q[i], na_g_k[i])
            tabs = _rope_tables(s)
            tm = 256
            qm, km, vm, qn, kn, vn = attn_inputs(
                h2, lambda j: j // (s // tm), lambda j: j % (s // tm), tm, gn1, sh1, sc1, wts, tabs, latent=True)
            tc = n_ctx
            kmc, vmc, knc, vnc = attn_inputs(
                ctx.reshape(b * n_ctx, d), lambda j: b, lambda j: 0, tc, gn1, sh1, sc1, wts,
                tuple(tb[:tc] for tb in tabs), latent=False)
            r3 = lambda arr, n: arr.reshape(b, n, arr.shape[-1])
            om = mla_attention(r3(qm, s), r3(km, s), r3(vm, s), r3(kmc, n_ctx), r3(vmc, n_ctx))
            on = na_attention(r3(qn, s), r3(kn, s), r3(vn, s), r3(knc, n_ctx), r3(vnc, n_ctx),
                              na_bias_table(na_rpb[i]))
            w_out = attn_w_out[i].astype(BF16)
            half = w_out.shape[0] // 2
            h2 = out_proj(om.reshape(t, -1), on.reshape(t, -1), w_out[:half], w_out[half:], h2, g1,
                          tiles_per_batch)
        else:
            a2 = conv_glu(h2, gn1, sh1, sc1, conv_w_pw1[i].astype(BF16), conv_b_pw1[i].reshape(1, -1),
                          tiles_per_batch)
            h2 = conv_tail(a2.reshape(b, s, d), conv_w_dw[i], conv_b_dw[i].reshape(1, d),
                           conv_g_ln[i].reshape(1, d), conv_b_ln[i].reshape(1, d),
                           conv_w_pw2[i].astype(BF16), conv_b_pw2[i].reshape(1, d),
                           h2.reshape(b, s, d), g1).reshape(t, d)
        h2 = peer_layer(h2, g_norm2[l].reshape(1, d), sh2, sc2, g2, peer_w_query[l], peer_sub_keys[l],
                        u_bf, vt_bf, l, tiles_per_batch)
    return h2.reshape(b, s, d)
```

```python
import functools

import jax
import jax.numpy as jnp
import numpy as np
from jax import lax
from jax.experimental import pallas as pl
from jax.experimental.pallas import tpu as pltpu

F32 = jnp.float32
BF16 = jnp.bfloat16

LANES = 128
SUBLANES = 8
EPS = 1e-6
NEG_BIG = -1e30
VMEM_LIMIT = 56 * 1024 * 1024

GRID_W = 64
HEAD_DIM = 64
N_HEADS = 8
MLA_NOPE = 64
MLA_ROPE = 32
MLA_QK = MLA_NOPE + MLA_ROPE
NA_WIN_ROWS = 8
NA_WIN_COLS = 16
ROPE_THETA = 10000.0
CONV_WIDTH = 31
CONV_HALO = 16
PEER_HEADS = 8
PEER_N_KEYS = 128
PEER_TOPK = 16

NT_DIMS = (((1,), (1,)), ((), ()))


def _params(*sem):
    return pltpu.CompilerParams(dimension_semantics=sem, vmem_limit_bytes=VMEM_LIMIT)


def _mm(a, b):
    return jnp.dot(a, b, preferred_element_type=F32)


def _mm_nt(a, b):
    return lax.dot_general(a, b, NT_DIMS, preferred_element_type=F32)


def _rms_mod(x, g, shift, scale):
    inv = lax.rsqrt(jnp.mean(x * x, axis=-1, keepdims=True) + EPS)
    return (x * inv * g) * (1.0 + scale) + shift


def _ada_kernel(cond_ref, w_ref, b_ref, out_ref):
    cond = cond_ref[...]
    a = cond * jax.nn.sigmoid(cond)
    out_ref[...] = jnp.dot(a, w_ref[...], preferred_element_type=F32,
                           precision=lax.Precision.HIGHEST) + b_ref[...]


def ada_params(cond, w_ada, b_ada, *, tn=1536):
    depth, d, n = w_ada.shape
    rows = cond.shape[0]
    return pl.pallas_call(
        _ada_kernel,
        grid=(depth, n // tn),
        in_specs=[pl.BlockSpec((rows, d), lambda l, j: (0, 0)),
                  pl.BlockSpec((None, d, tn), lambda l, j: (l, 0, j)),
                  pl.BlockSpec((None, 1, tn), lambda l, j: (l, 0, j))],
        out_specs=pl.BlockSpec((None, rows, tn), lambda l, j: (l, 0, j)),
        out_shape=jax.ShapeDtypeStruct((depth, rows, n), F32),
        compiler_params=_params("parallel", "parallel"),
        name="ada_params",
    )(cond, w_ada, b_ada.reshape(depth, 1, n))


def _rope128(x, cos, sin):
    return x * cos + pltpu.roll(x, LANES // 2, 1) * sin


def _head_rms128(x, n_valid):
    ss = jnp.sum(x * x, axis=-1, keepdims=True) * (1.0 / n_valid)
    return x * lax.rsqrt(ss + EPS)


def _pair_rms64(x, lane_lo):
    xx = x * x
    ss_lo = jnp.sum(jnp.where(lane_lo, xx, 0.0), axis=-1, keepdims=True)
    ss_hi = jnp.sum(jnp.where(lane_lo, 0.0, xx), axis=-1, keepdims=True)
    inv = jnp.where(lane_lo, lax.rsqrt(ss_lo * (1.0 / HEAD_DIM) + EPS), lax.rsqrt(ss_hi * (1.0 / HEAD_DIM) + EPS))
    return x * inv


def _attn_in_kernel(x_ref, g1_ref, sh_ref, sc_ref, w_in_ref, gqa_ref, wq_ref, gkva_ref, wk_ref, wv_ref,
                    gq_ref, gk_ref, gnq_ref, gnk_ref, cos_ref, sin_ref, *out_refs, latent):
    if latent:
        qm_ref, km_ref, vm_ref, qn_ref, kn_ref, vn_ref = out_refs
    else:
        km_ref, vm_ref, kn_ref, vn_ref = out_refs
    a = _rms_mod(x_ref[...], g1_ref[...], sh_ref[...], sc_ref[...]).astype(BF16)
    z = _mm(a, w_in_ref[...])
    c0, c1, c2 = 384, 384 + 256, 384 + 256 + LANES
    cq, ckv, kr = z[:, :c0], z[:, c0:c1], z[:, c1:c2]
    naq, nak, nav = z[:, c2:c2 + 512], z[:, c2 + 512:c2 + 1024], z[:, c2 + 1024:c2 + 1536]

    def rms(v, g):
        return v * lax.rsqrt(jnp.mean(v * v, axis=-1, keepdims=True) + EPS) * g

    ckv_n = rms(ckv, gkva_ref[...]).astype(BF16)
    kfull = _mm(ckv_n, wk_ref[...])
    vm_ref[...] = _mm(ckv_n, wv_ref[...]).astype(BF16)
    if latent:
        qfull = _mm(rms(cq, gqa_ref[...]).astype(BF16), wq_ref[...])
        cos, sin = cos_ref[...], sin_ref[...]
    gq, gk = gq_ref[...], gk_ref[...]
    for h in range(N_HEADS):
        sl = slice(h * LANES, (h + 1) * LANES)
        kh = _head_rms128(kfull[:, sl] + kr, MLA_QK) * gk
        if latent:
            kh = _rope128(kh, cos, sin)
            qh = _rope128(_head_rms128(qfull[:, sl], MLA_QK) * gq, cos, sin)
            qm_ref[:, sl] = qh.astype(BF16)
        km_ref[:, sl] = kh.astype(BF16)
    lane_lo = lax.broadcasted_iota(jnp.int32, (1, LANES), 1) < HEAD_DIM
    gnq, gnk = gnq_ref[...], gnk_ref[...]
    for p in range(N_HEADS // 2):
        sl = slice(p * LANES, (p + 1) * LANES)
        kn_ref[:, sl] = (_pair_rms64(nak[:, sl], lane_lo) * gnk).astype(BF16)
        if latent:
            qn_ref[:, sl] = (_pair_rms64(naq[:, sl], lane_lo) * gnq).astype(BF16)
    vn_ref[...] = nav.astype(BF16)


def attn_inputs(x2, batch_of_tile, pos_of_tile, tm, g1, sh, sc, wts, rope_tabs, *, latent):
    t, d = x2.shape
    w_in, gqa, wq, gkva, wk, wv, gq, gk, gnq, gnk = wts
    full = lambda arr: pl.BlockSpec(arr.shape, lambda i: (0,) * arr.ndim)
    row = pl.BlockSpec((None, 1, d), lambda i: (batch_of_tile(i), 0, 0))
    tab = pl.BlockSpec((tm, LANES), lambda i: (pos_of_tile(i), 0))
    wide = jax.ShapeDtypeStruct((t, N_HEADS * LANES), BF16)
    half = jax.ShapeDtypeStruct((t, N_HEADS * HEAD_DIM), BF16)
    o_wide = pl.BlockSpec((tm, N_HEADS * LANES), lambda i: (i, 0))
    o_half = pl.BlockSpec((tm, N_HEADS * HEAD_DIM), lambda i: (i, 0))
    if latent:
        out_shape, out_specs = (wide, wide, half, half, half, half), (o_wide, o_wide, o_half, o_half, o_half, o_half)
    else:
        out_shape, out_specs = (wide, half, half, half), (o_wide, o_half, o_half, o_half)
    return pl.pallas_call(
        functools.partial(_attn_in_kernel, latent=latent),
        grid=(t // tm,),
        in_specs=[pl.BlockSpec((tm, d), lambda i: (i, 0)), full(g1), row, row,
                  full(w_in), full(gqa), full(wq), full(gkva), full(wk), full(wv),
                  full(gq), full(gk), full(gnq), full(gnk), tab, tab],
        out_specs=out_specs,
        out_shape=out_shape,
        compiler_params=_params("parallel"),
        name="attn_inputs_lat" if latent else "attn_inputs_ctx",
    )(x2, g1, sh, sc, w_in, gqa, wq, gkva, wk, wv, gq, gk, gnq, gnk, *rope_tabs)


def _softmax_pv(s_list, v_list):
    m = functools.reduce(jnp.maximum, [jnp.max(s, axis=-1, keepdims=True) for s in s_list])
    ps = [jnp.exp(s - m) for s in s_list]
    denom = functools.reduce(jnp.add, [jnp.sum(p, axis=-1, keepdims=True) for p in ps])
    o = functools.reduce(jnp.add, [_mm(p.astype(BF16), v) for p, v in zip(ps, v_list)])
    return o / denom


def _mla_kernel(q_ref, k_ref, v_ref, kc_ref, vc_ref, o_ref):
    lane_lo = lax.broadcasted_iota(jnp.int32, (1, LANES), 1) < HEAD_DIM
    for p in range(N_HEADS // 2):
        vsl = slice(p * LANES, (p + 1) * LANES)
        outs = []
        for h in (2 * p, 2 * p + 1):
            sl = slice(h * LANES, (h + 1) * LANES)
            q = q_ref[:, sl]
            outs.append(_softmax_pv([_mm_nt(q, k_ref[:, sl]), _mm_nt(q, kc_ref[:, sl])],
                                    [v_ref[:, vsl], vc_ref[:, vsl]]))
        o_ref[:, vsl] = jnp.where(lane_lo, outs[0], outs[1]).astype(BF16)


def mla_attention(qm, km, vm, kmc, vmc, *, tq=512):
    b, s, dq = qm.shape
    c, dv = kmc.shape[1], vm.shape[2]
    return pl.pallas_call(
        _mla_kernel,
        grid=(b, s // tq),
        in_specs=[pl.BlockSpec((None, tq, dq), lambda i, j: (i, j, 0)),
                  pl.BlockSpec((None, s, dq), lambda i, j: (i, 0, 0)),
                  pl.BlockSpec((None, s, dv), lambda i, j: (i, 0, 0)),
                  pl.BlockSpec((None, c, dq), lambda i, j: (i, 0, 0)),
                  pl.BlockSpec((None, c, dv), lambda i, j: (i, 0, 0))],
        out_specs=pl.BlockSpec((None, tq, dv), lambda i, j: (i, j, 0)),
        out_shape=jax.ShapeDtypeStruct((b, s, dv), BF16),
        compiler_params=_params("parallel", "arbitrary"),
        name="mla_attention",
    )(qm, km, vm, kmc, vmc)


def _na_row0(r, n_rows):
    return jnp.clip(r - NA_WIN_ROWS // 2, 0, n_rows - NA_WIN_ROWS)


def _na_kernel(q_ref, k_ref, v_ref, kc_ref, vc_ref, bias_ref, o_ref, *, n_rows, rows_per_step):
    band = NA_WIN_ROWS * GRID_W
    lane_lo = lax.broadcasted_iota(jnp.int32, (1, LANES), 1) < HEAD_DIM
    for rr in range(rows_per_step):
        r = pl.program_id(1) * rows_per_step + rr
        row0 = _na_row0(r, n_rows)
        start = pl.multiple_of(row0 * GRID_W, GRID_W)
        cls = r - row0
        qrows = slice(rr * GRID_W, (rr + 1) * GRID_W)
        for p in range(N_HEADS // 2):
            sl = slice(p * LANES, (p + 1) * LANES)
            qp = q_ref[qrows, sl]
            kp, vp = k_ref[pl.ds(start, band), sl], v_ref[pl.ds(start, band), sl]
            kcp, vcp = kc_ref[:, sl], vc_ref[:, sl]
            zero = jnp.zeros_like(qp)
            q2 = jnp.concatenate([jnp.where(lane_lo, qp, zero), jnp.where(lane_lo, zero, qp)], axis=0)
            bias2 = jnp.concatenate([bias_ref[cls, 2 * p], bias_ref[cls, 2 * p + 1]], axis=0)
            o2 = _softmax_pv([_mm_nt(q2, kp) + bias2, _mm_nt(q2, kcp)], [vp, vcp])
            o_ref[qrows, sl] = jnp.where(lane_lo, o2[:GRID_W], o2[GRID_W:]).astype(BF16)


def na_attention(qn, kn, vn, knc, vnc, bias, *, rows_per_step=4):
    b, s, dh = qn.shape
    c = knc.shape[1]
    n_rows = s // GRID_W
    tq = rows_per_step * GRID_W
    whole = lambda n: pl.BlockSpec((None, n, dh), lambda i, r: (i, 0, 0))
    return pl.pallas_call(
        functools.partial(_na_kernel, n_rows=n_rows, rows_per_step=rows_per_step),
        grid=(b, n_rows // rows_per_step),
        in_specs=[pl.BlockSpec((None, tq, dh), lambda i, r: (i, r, 0)),
                  whole(s), whole(s), whole(c), whole(c),
                  pl.BlockSpec(bias.shape, lambda i, r: (0, 0, 0, 0))],
        out_specs=pl.BlockSpec((None, tq, dh), lambda i, r: (i, r, 0)),
        out_shape=jax.ShapeDtypeStruct((b, s, dh), BF16),
        compiler_params=_params("parallel", "arbitrary"),
        name="na_attention",
    )(qn, kn, vn, knc, vnc, bias)


def na_bias_table(rpb):
    n_heads, n_rel_r, n_rel_c = rpb.shape
    q = np.arange(GRID_W)
    c0 = np.clip(q - NA_WIN_COLS // 2, 0, GRID_W - NA_WIN_COLS)
    kc = np.arange(GRID_W)
    inside = (kc[None, :] >= c0[:, None]) & (kc[None, :] < c0[:, None] + NA_WIN_COLS)
    rel_c = kc[None, :] - q[:, None] + NA_WIN_COLS - 1
    onehot = ((rel_c[None] == np.arange(n_rel_c)[:, None, None]) & inside[None]).astype(np.float32)
    toep = jnp.einsum("hrc,cqk->hrqk", rpb, jnp.asarray(onehot), precision=lax.Precision.HIGHEST)
    toep = jnp.where(inside[None, None], toep, NEG_BIG)
    per_cls = [toep[:, NA_WIN_ROWS - 1 - cls:2 * NA_WIN_ROWS - 1 - cls] for cls in range(NA_WIN_ROWS)]
    tab = jnp.stack(per_cls)
    return tab.transpose(0, 1, 3, 2, 4).reshape(NA_WIN_ROWS, n_heads, GRID_W, NA_WIN_ROWS * GRID_W)


def _out_proj_kernel(om_ref, on_ref, wa_ref, wb_ref, h_ref, gate_ref, out_ref):
    y = _mm(om_ref[...], wa_ref[...]) + _mm(on_ref[...], wb_ref[...])
    out_ref[...] = h_ref[...] + gate_ref[...] * y


def out_proj(om, on, wa, wb, h2, gate, tiles_per_batch, *, tm=512):
    t, d = h2.shape
    k = om.shape[1]
    return pl.pallas_call(
        _out_proj_kernel,
        grid=(t // tm,),
        in_specs=[pl.BlockSpec((tm, k), lambda i: (i, 0)), pl.BlockSpec((tm, k), lambda i: (i, 0)),
                  pl.BlockSpec((k, d), lambda i: (0, 0)), pl.BlockSpec((k, d), lambda i: (0, 0)),
                  pl.BlockSpec((tm, d), lambda i: (i, 0)),
                  pl.BlockSpec((None, 1, d), lambda i: (i // tiles_per_batch(tm), 0, 0))],
        out_specs=pl.BlockSpec((tm, d), lambda i: (i, 0)),
        out_shape=jax.ShapeDtypeStruct((t, d), F32),
        compiler_params=_params("parallel"),
        name="out_proj",
    )(om, on, wa, wb, h2, gate)


def _sort16_comparators():
    comps = []

    def merge(lo, n, r):
        step = r * 2
        if step < n:
            merge(lo, n, step)
            merge(lo + r, n, step)
            comps.extend((i, i + r) for i in range(lo + r, lo + n - r, step))
        else:
            comps.append((lo, lo + r))

    def sort(lo, n):
        if n > 1:
            sort(lo, n // 2)
            sort(lo + n // 2, n // 2)
            merge(lo, n, 1)

    sort(0, PEER_TOPK)
    return comps


def _cmp_exchange(x, i, j):
    a, b = x[i], x[j]
    if a is None:
        x[i], x[j] = b, None
    elif b is not None:
        x[i], x[j] = jnp.maximum(a, b), jnp.minimum(a, b)


def _top16_of_slabs(slabs):
    x = list(slabs) + [None] * (PEER_TOPK - len(slabs))
    for i, j in _sort16_comparators():
        _cmp_exchange(x, i, j)
    for shift in (4, 2, 1):
        y = [None if v is None else pltpu.roll(v, shift, 0) for v in x]
        z = []
        for v in range(PEER_TOPK):
            a, b = x[v], y[PEER_TOPK - 1 - v]
            z.append(b if a is None else a if b is None else jnp.maximum(a, b))
        d = PEER_TOPK // 2
        while d:
            for i in range(PEER_TOPK):
                if i & d == 0:
                    _cmp_exchange(z, i, i + d)
            d //= 2
        x = z
    return x


def _on_sublanes(slabs, sub):
    out = slabs[-1]
    for s in range(len(slabs) - 2, -1, -1):
        out = jnp.where(sub == s, slabs[s], out)
    return out


def _peer_router_kernel(x_ref, g_ref, sh_ref, sc_ref, wq_ref, keys_ref, ht_ref, sb_ref, eb_ref, th_ref, ea_ref):
    a = _rms_mod(x_ref[...], g_ref[...], sh_ref[...], sc_ref[...])
    at = a.T.astype(BF16)
    ht_ref[...] = at
    qt = _mm(wq_ref[...], at).astype(BF16)
    k = PEER_TOPK
    n_slabs = PEER_N_KEYS // SUBLANES
    sub = lax.broadcasted_iota(jnp.int32, (SUBLANES, 1), 0)
    slab = lambda s, v: s[v * SUBLANES:(v + 1) * SUBLANES]
    for h in range(PEER_HEADS):
        sa = _mm(keys_ref[2 * h], qt[(2 * h) * LANES:(2 * h + 1) * LANES])
        sb = _mm(keys_ref[2 * h + 1], qt[(2 * h + 1) * LANES:(2 * h + 2) * LANES])
        top_a = _top16_of_slabs([slab(sa, v) for v in range(n_slabs)])
        top_b = _top16_of_slabs([slab(sb, v) for v in range(n_slabs)])
        a_lo, a_hi = _on_sublanes(top_a[:SUBLANES], sub), _on_sublanes(top_a[SUBLANES:], sub)
        b_hi = _on_sublanes(top_b[SUBLANES:], sub)
        cands = [a_lo + top_b[0], a_hi + top_b[0]]
        cands += [jnp.where(sub < k // (rj + 1), a_lo + top_b[rj], NEG_BIG) for rj in range(1, SUBLANES)]
        cands.append(top_a[0] + b_hi)
        best = _top16_of_slabs(cands)
        tau, m = best[k - 1], best[0]
        z = functools.reduce(jnp.add, [jnp.exp(c - m) for c in best])
        theta = [jnp.full_like(slab(sa, 0), -NEG_BIG) for _ in range(n_slabs)]
        n_cut = 4
        for rj in range(n_cut):
            if rj == 0:
                ok = jnp.minimum(jnp.where(cands[0] >= tau, a_lo, -NEG_BIG), jnp.where(cands[1] >= tau, a_hi, -NEG_BIG))
            else:
                ok = jnp.where(cands[rj + 1] >= tau, a_lo, -NEG_BIG)
            for shift in (4, 2, 1):
                ok = jnp.minimum(ok, pltpu.roll(ok, shift, 0))
            theta = [jnp.minimum(th, jnp.where(slab(sa, v) >= ok, top_b[rj], -NEG_BIG)) for v, th in enumerate(theta)]
        for ri in range(k // (n_cut + 1)):
            low = [jnp.where(top_a[ri] + top_b[rj] >= tau, top_b[rj], -NEG_BIG)
                   for rj in range(n_cut, k) if (ri + 1) * (rj + 1) <= k]
            low = functools.reduce(jnp.minimum, low)
            theta = [jnp.minimum(th, jnp.where(slab(sa, v) == top_a[ri], low, -NEG_BIG)) for v, th in enumerate(theta)]
        inv_z = 1.0 / z
        for v in range(n_slabs):
            rows = slice(v * SUBLANES, (v + 1) * SUBLANES)
            sb_ref[h, rows, :] = slab(sb, v)
            eb_ref[h, rows, :] = jnp.exp(slab(sb, v) - top_b[0]) * inv_z
            th_ref[h, rows, :] = theta[v]
            ea_ref[h, rows, :] = jnp.exp(slab(sa, v) - top_a[0])


def peer_router(h2, g, sh, sc, wq_t, keys, tiles_per_batch, *, tt=512):
    t, d = h2.shape
    tab = lambda dtype: jax.ShapeDtypeStruct((PEER_HEADS, PEER_N_KEYS, t), dtype)
    tab_spec = pl.BlockSpec((PEER_HEADS, PEER_N_KEYS, tt), lambda i: (0, 0, i))
    row = pl.BlockSpec((None, 1, d), lambda i: (i // tiles_per_batch(tt), 0, 0))
    return pl.pallas_call(
        _peer_router_kernel,
        grid=(t // tt,),
        in_specs=[pl.BlockSpec((tt, d), lambda i: (i, 0)), pl.BlockSpec((1, d), lambda i: (0, 0)), row, row,
                  pl.BlockSpec(wq_t.shape, lambda i: (0, 0)), pl.BlockSpec(keys.shape, lambda i: (0, 0, 0))],
        out_specs=(pl.BlockSpec((d, tt), lambda i: (0, i)), tab_spec, tab_spec, tab_spec, tab_spec),
        out_shape=(jax.ShapeDtypeStruct((d, t), BF16), tab(F32), tab(F32), tab(F32), tab(F32)),
        compiler_params=_params("parallel"),
        name="peer_router",
    )(h2, g, sh, sc, wq_t, keys)


def _gelu_exact(x):
    return 0.5 * x * (1.0 + lax.erf(x * np.float32(np.sqrt(0.5))))


def _peer_expert_kernel(ht_ref, u_ref, vt_ref, sb_ref, eb_ref, th_ref, ea_ref, h_ref, gate_ref, out_ref,
                        acc_ref, w_ref, *, rows_per_block):
    e = pl.program_id(1)
    tt = acc_ref.shape[1]
    ipc = 8
    n_chunks = rows_per_block // ipc
    rows_c = ipc * PEER_N_KEYS
    jt = 2 * SUBLANES

    def score_chunk(c):
        r0 = c * rows_c
        act = _gelu_exact(_mm(u_ref[r0:r0 + rows_c, :], ht_ref[...]))
        for lg in range(tt // LANES):
            lanes = slice(lg * LANES, (lg + 1) * LANES)
            for n in range(ipc):
                il = ipc * c + n
                rows = [(jnp.maximum(jnp.broadcast_to(th_ref[h, il:il + 1, lanes], (SUBLANES, LANES)), NEG_BIG),
                         jnp.maximum(jnp.broadcast_to(ea_ref[h, il:il + 1, lanes], (SUBLANES, LANES)), 0.0))
                        for h in range(PEER_HEADS)]
                for j0 in range(0, PEER_N_KEYS, jt):
                    halves = []
                    for k in range(jt // SUBLANES):
                        js = slice(j0 + k * SUBLANES, j0 + (k + 1) * SUBLANES)
                        g = None
                        for h, (theta, ea) in enumerate(rows):
                            term = jnp.where(sb_ref[h, js, lanes] >= theta, eb_ref[h, js, lanes], 0.0) * ea
                            g = term if g is None else g + term
                        off = n * PEER_N_KEYS + j0 + k * SUBLANES
                        halves.append(g * act[off:off + SUBLANES, lanes])
                    off = n * PEER_N_KEYS + j0
                    w_ref[r0 + off:r0 + off + jt, lanes] = jnp.concatenate(halves, axis=0).astype(BF16)

    @pl.when(e == 0)
    def _():
        acc_ref[...] = jnp.zeros_like(acc_ref)

    for c in range(n_chunks):
        score_chunk(c)
    acc_ref[...] += _mm(vt_ref[...], w_ref[...])

    @pl.when(e == pl.num_programs(1) - 1)
    def _():
        out_ref[...] = h_ref[...] + gate_ref[...] * acc_ref[...].T


def peer_experts(ht, u_bf, vt_bf, layer, sb, eb, th, ea, h2, gate, tiles_per_batch, *, tt=512, te=2048):
    t, d = h2.shape
    n_blocks = u_bf.shape[1] // te
    rows = te // PEER_N_KEYS
    full_tab = pl.BlockSpec((PEER_HEADS, PEER_N_KEYS, tt), lambda i, e: (0, 0, i))
    row_tab = pl.BlockSpec((PEER_HEADS, rows, tt), lambda i, e: (0, e, i))
    return pl.pallas_call(
        functools.partial(_peer_expert_kernel, rows_per_block=rows),
        grid=(t // tt, n_blocks),
        in_specs=[pl.BlockSpec((d, tt), lambda i, e: (0, i)),
                  pl.BlockSpec((None, te, d), lambda i, e: (layer, e, 0)),
                  pl.BlockSpec((None, d, te), lambda i, e: (layer, 0, e)),
                  full_tab, full_tab, row_tab, row_tab,
                  pl.BlockSpec((tt, d), lambda i, e: (i, 0)),
                  pl.BlockSpec((None, 1, d), lambda i, e: (i // tiles_per_batch(tt), 0, 0))],
        out_specs=pl.BlockSpec((tt, d), lambda i, e: (i, 0)),
        out_shape=jax.ShapeDtypeStruct((t, d), F32),
        scratch_shapes=[pltpu.VMEM((d, tt), F32), pltpu.VMEM((te, tt), BF16)],
        compiler_params=_params("parallel", "arbitrary"),
        name="peer_experts",
    )(ht, u_bf, vt_bf, sb, eb, th, ea, h2, gate)


def peer_layer(h2, g, sh, sc, gate, w_query, sub_keys, u_bf, vt_bf, layer, tiles_per_batch):
    wq_t = w_query.T.astype(BF16)
    keys = sub_keys.reshape(PEER_HEADS * 2, PEER_N_KEYS, -1).astype(BF16)
    ht, sb, eb, th, ea = peer_router(h2, g, sh, sc, wq_t, keys, tiles_per_batch)
    return peer_experts(ht, u_bf, vt_bf, layer, sb, eb, th, ea, h2, gate, tiles_per_batch)


def _glu_kernel(x_ref, g_ref, sh_ref, sc_ref, w_ref, b_ref, out_ref):
    a = _rms_mod(x_ref[...], g_ref[...], sh_ref[...], sc_ref[...]).astype(BF16)
    y = _mm(a, w_ref[...]) + b_ref[...]
    d = out_ref.shape[-1]
    out_ref[...] = y[:, :d] * jax.nn.sigmoid(y[:, d:])


def conv_glu(h2, g, sh, sc, w_pw1, b_pw1, tiles_per_batch, *, tm=512):
    t, d = h2.shape
    row = pl.BlockSpec((None, 1, d), lambda i: (i // tiles_per_batch(tm), 0, 0))
    return pl.pallas_call(
        _glu_kernel,
        grid=(t // tm,),
        in_specs=[pl.BlockSpec((tm, d), lambda i: (i, 0)), pl.BlockSpec((1, d), lambda i: (0, 0)), row, row,
                  pl.BlockSpec((d, 2 * d), lambda i: (0, 0)), pl.BlockSpec((1, 2 * d), lambda i: (0, 0))],
        out_specs=pl.BlockSpec((tm, d), lambda i: (i, 0)),
        out_shape=jax.ShapeDtypeStruct((t, d), F32),
        compiler_params=_params("parallel"),
        name="conv_glu",
    )(h2, g, sh, sc, w_pw1, b_pw1)


def _conv_tail_kernel(prev_ref, cur_ref, next_ref, wdw_ref, bdw_ref, gln_ref, bln_ref, w2_ref, b2_ref,
                      h_ref, gate_ref, out_ref, win_ref, shift_ref, *, ts):
    s = pl.program_id(1)
    halo = CONV_HALO
    win_ref[0:halo, :] = jnp.where(s > 0, prev_ref[...], 0.0)
    win_ref[halo:halo + ts, :] = cur_ref[...]
    win_ref[halo + ts:, :] = jnp.where(s < pl.num_programs(1) - 1, next_ref[...], 0.0)
    acc = jnp.zeros((ts, cur_ref.shape[-1]), F32) + bdw_ref[...]
    base = halo - CONV_WIDTH // 2
    span = shift_ref.shape[0]
    for r in range(SUBLANES):
        src = win_ref
        if r:
            shift_ref[...] = win_ref[r:r + span, :]
            src = shift_ref
        for a in range(0, span - ts + 1, SUBLANES):
            k = a + r - base
            if 0 <= k < CONV_WIDTH:
                acc = acc + src[a:a + ts, :] * wdw_ref[k:k + 1, :]
    mu = jnp.mean(acc, axis=-1, keepdims=True)
    cen = acc - mu
    var = jnp.mean(cen * cen, axis=-1, keepdims=True)
    y = cen * lax.rsqrt(var + EPS) * gln_ref[...] + bln_ref[...]
    y = y * jax.nn.sigmoid(y)
    y = _mm(y.astype(BF16), w2_ref[...]) + b2_ref[...]
    out_ref[...] = h_ref[...] + gate_ref[...] * y


def conv_tail(a3, w_dw, b_dw, g_ln, b_ln, w_pw2, b_pw2, h3, gate, *, ts=256):
    b, s, d = a3.shape
    hb = ts // CONV_HALO
    n_halo = s // CONV_HALO
    vec = pl.BlockSpec((1, d), lambda i, j: (0, 0))
    return pl.pallas_call(
        functools.partial(_conv_tail_kernel, ts=ts),
        grid=(b, s // ts),
        in_specs=[pl.BlockSpec((None, CONV_HALO, d), lambda i, j: (i, jnp.maximum(j * hb - 1, 0), 0)),
                  pl.BlockSpec((None, ts, d), lambda i, j: (i, j, 0)),
                  pl.BlockSpec((None, CONV_HALO, d), lambda i, j: (i, jnp.minimum((j + 1) * hb, n_halo - 1), 0)),
                  pl.BlockSpec((CONV_WIDTH, d), lambda i, j: (0, 0)), vec, vec, vec,
                  pl.BlockSpec((d, d), lambda i, j: (0, 0)), vec,
                  pl.BlockSpec((None, ts, d), lambda i, j: (i, j, 0)),
                  pl.BlockSpec((None, 1, d), lambda i, j: (i, 0, 0))],
        out_specs=pl.BlockSpec((None, ts, d), lambda i, j: (i, j, 0)),
        out_shape=jax.ShapeDtypeStruct((b, s, d), F32),
        scratch_shapes=[pltpu.VMEM((ts + 2 * CONV_HALO, d), F32),
                        pltpu.VMEM((ts + 2 * CONV_HALO - SUBLANES, d), F32)],
        compiler_params=_params("parallel", "arbitrary"),
        name="conv_tail",
    )(a3, a3, a3, w_dw, b_dw, g_ln, b_ln, w_pw2, b_pw2, h3, gate)


def _mla_lanes(w):
    half = MLA_ROPE // 2
    cut = LANES // 2 - half
    pad = jnp.zeros(w.shape[:-1] + (LANES - MLA_QK,), w.dtype)
    return jnp.concatenate([w[..., MLA_NOPE:MLA_NOPE + half], w[..., :cut], w[..., MLA_NOPE + half:MLA_QK],
                            w[..., cut:MLA_NOPE], pad], axis=-1)


def _attn_weights(w_in, g_qa, w_q_up, g_kva, w_kv_up, g_q, g_k, na_g_q, na_g_k):
    a, b, c = 384, 384 + 256, 384 + 256 + MLA_ROPE
    rows = w_in.shape[0]
    kr_cols = _mla_lanes(jnp.concatenate([jnp.zeros((rows, MLA_NOPE), w_in.dtype), w_in[:, b:c]], axis=1))
    w_in_r = jnp.concatenate([w_in[:, :a], w_in[:, a:b], kr_cols, w_in[:, c:]], axis=1).astype(BF16)
    wq = _mla_lanes(w_q_up.reshape(-1, N_HEADS, MLA_QK)).reshape(-1, N_HEADS * LANES).astype(BF16)
    kv = w_kv_up.reshape(-1, N_HEADS, MLA_NOPE + HEAD_DIM)
    k_nope = jnp.pad(kv[:, :, :MLA_NOPE], ((0, 0), (0, 0), (0, MLA_ROPE)))
    wk = _mla_lanes(k_nope).reshape(-1, N_HEADS * LANES).astype(BF16)
    wv = kv[:, :, MLA_NOPE:].reshape(-1, N_HEADS * HEAD_DIM).astype(BF16)
    pad_g = lambda g: _mla_lanes(g).reshape(1, LANES)
    gq = pad_g(g_q) * (MLA_QK ** -0.5)
    gk = pad_g(g_k)
    gnq = jnp.tile(na_g_q, 2).reshape(1, LANES) * (HEAD_DIM ** -0.5)
    gnk = jnp.tile(na_g_k, 2).reshape(1, LANES)
    return (w_in_r, g_qa.reshape(1, -1), wq, g_kva.reshape(1, -1), wk, wv, gq, gk, gnq, gnk)


def _rope_tables(s):
    t = jnp.arange(s)
    row = (t // GRID_W).astype(F32)
    col = (t % GRID_W).astype(F32)
    n_freq = MLA_ROPE // 4
    inv_freq = 1.0 / (ROPE_THETA ** (jnp.arange(n_freq, dtype=F32) / n_freq))
    ang = jnp.concatenate([row[:, None] * inv_freq, col[:, None] * inv_freq], axis=-1)
    cos, sin = jnp.cos(ang), jnp.sin(ang)
    cos_t = _mla_lanes(jnp.concatenate([jnp.ones((s, MLA_NOPE), F32), cos, cos], axis=1))
    sin_t = _mla_lanes(jnp.concatenate([jnp.zeros((s, MLA_NOPE), F32), -sin, sin], axis=1))
    return cos_t, sin_t


def kernel(x, c, ctx, c_ctx, w_ada, b_ada, g_norm1, g_norm2, attn_w_in, mla_g_qa, mla_w_q_up, mla_g_kva, mla_w_kv_up, mla_g_q, mla_g_k, na_g_q, na_g_k, na_rpb, attn_w_out, conv_w_pw1, conv_b_pw1, conv_w_dw, conv_b_dw, conv_g_ln, conv_b_ln, conv_w_pw2, conv_b_pw2, peer_w_query, peer_sub_keys, peer_u, peer_v):
    b, s, d = x.shape
    n_ctx = ctx.shape[1]
    t = b * s
    depth = w_ada.shape[0]
    assert depth == 2
    tiles_per_batch = lambda tile: s // tile

    cond = jnp.zeros((16, d), F32).at[:b].set(c).at[b].set(c_ctx)
    mod = ada_params(cond, w_ada, b_ada).reshape(depth, 16, 6, 1, d)
    mods = [[mod[l, :, k] for k in range(6)] for l in range(depth)]

    u_bf = peer_u.astype(BF16)
    vt_bf = jnp.swapaxes(peer_v, 1, 2).astype(BF16)

    h2 = x.reshape(t, d)
    for l in range(depth):
        sh1, sc1, g1, sh2, sc2, g2 = mods[l]
        i = l // 2
        gn1 = g_norm1[l].reshape(1, d)
        if l % 2 == 0:
            wts = _attn_weights(attn_w_in[i], mla_g_qa[i], mla_w_q_up[i], mla_g_kva[i], mla_w_kv_up[i],
                                mla_g_q[i], mla_g_k[i], na_g_q[i], na_g_k[i])
            tabs = _rope_tables(s)
            tm = 256
            qm, km, vm, qn, kn, vn = attn_inputs(
                h2, lambda j: j // (s // tm), lambda j: j % (s // tm), tm, gn1, sh1, sc1, wts, tabs, latent=True)
            tc = n_ctx
            kmc, vmc, knc, vnc = attn_inputs(
                ctx.reshape(b * n_ctx, d), lambda j: b, lambda j: 0, tc, gn1, sh1, sc1, wts,
                tuple(tb[:tc] for tb in tabs), latent=False)
            r3 = lambda arr, n: arr.reshape(b, n, arr.shape[-1])
            om = mla_attention(r3(qm, s), r3(km, s), r3(vm, s), r3(kmc, n_ctx), r3(vmc, n_ctx))
            on = na_attention(r3(qn, s), r3(kn, s), r3(vn, s), r3(knc, n_ctx), r3(vnc, n_ctx),
                              na_bias_table(na_rpb[i]))
            w_out = attn_w_out[i].astype(BF16)
            half = w_out.shape[0] // 2
            h2 = out_proj(om.reshape(t, -1), on.reshape(t, -1), w_out[:half], w_out[half:], h2, g1,
                          tiles_per_batch)
        else:
            a2 = conv_glu(h2, gn1, sh1, sc1, conv_w_pw1[i].astype(BF16), conv_b_pw1[i].reshape(1, -1),
                          tiles_per_batch)
            h2 = conv_tail(a2.reshape(b, s, d), conv_w_dw[i], conv_b_dw[i].reshape(1, d),
                           conv_g_ln[i].reshape(1, d), conv_b_ln[i].reshape(1, d),
                           conv_w_pw2[i].astype(BF16), conv_b_pw2[i].reshape(1, d),
                           h2.reshape(b, s, d), g1).reshape(t, d)
        h2 = peer_layer(h2, g_norm2[l].reshape(1, d), sh2, sc2, g2, peer_w_query[l], peer_sub_keys[l],
                        u_bf, vt_bf, l, tiles_per_batch)
    return h2.reshape(b, s, d)
```
